```python
import jax
import jax.numpy as jnp
from jax import lax

D_MODEL = 1024
BATCH = 8
SEQ = 2048
DEPTH = 1

CHUNK = 64
EPS = 1e-6
CONV_CH = D_MODEL
CONV_WIDTH = 31
SGU_CH = D_MODEL
SGU_GROUPS = 8
SGU_GROUP_CH = SGU_CH // SGU_GROUPS
SGU_BLOCK = 128
IN_WIDTH = 2 * CONV_CH + 2 * SGU_CH + 2 * D_MODEL
PEER_HEADS = 8
PEER_N_KEYS = 128
PEER_N_EXPERTS = PEER_N_KEYS * PEER_N_KEYS
PEER_TOPK = 16
PEER_D_KEY = 256
PEER_D_HALF = PEER_D_KEY // 2
PEER_TOKEN_BLOCK = 128

kernel_name = "hybrid_conv_sgu_peer_adaln_block"


def rms_norm(x, g):
    xf = x.astype(jnp.float32)
    y = xf * lax.rsqrt(jnp.mean(xf * xf, axis=-1, keepdims=True) + EPS)
    return (y * g.astype(jnp.float32)).astype(x.dtype)


def layer_norm(x, g, b):
    xf = x.astype(jnp.float32)
    mu = jnp.mean(xf, axis=-1, keepdims=True)
    var = jnp.mean(jnp.square(xf - mu), axis=-1, keepdims=True)
    y = (xf - mu) * lax.rsqrt(var + EPS)
    return (y * g.astype(jnp.float32) + b.astype(jnp.float32)).astype(x.dtype)


def modulate(n, shift, scale):
    return n * (1 + scale[:, None, :]) + shift[:, None, :]


def conv_branch(a_val, a_gate, dw_w, dw_b, ln_g, ln_b, w_proj):
    z = a_val * jax.nn.sigmoid(a_gate)
    z = lax.conv_general_dilated(
        z, dw_w[:, None, :], window_strides=(1,),
        padding=[(CONV_WIDTH - 1, 0)],
        dimension_numbers=("NWC", "WIO", "NWC"),
        feature_group_count=CONV_CH) + dw_b
    z = jax.nn.silu(layer_norm(z, ln_g, ln_b))
    return z @ w_proj


def sgu_branch(u, v, ln_g, ln_b, w_s, b_s, w_proj):
    B, S, _ = u.shape
    v = layer_norm(v, ln_g, ln_b)
    pos_chunk = jnp.arange(SGU_BLOCK) // CHUNK
    mask = pos_chunk[None, :] <= pos_chunk[:, None]
    w_masked = jnp.where(mask[None], w_s, 0)
    vb = v.reshape(B, S // SGU_BLOCK, SGU_BLOCK, SGU_GROUPS, SGU_GROUP_CH)
    mixed = jnp.einsum("gij,bnjgc->bnigc", w_masked, vb) + b_s.T[None, None, :, :, None]
    gated = u * mixed.reshape(B, S, SGU_CH)
    return gated @ w_proj


def peer(n, w_query, sub_keys, expert_u, expert_v):
    B, S, D = n.shape
    T = B * S
    xt = n.reshape(T, D)
    q = (xt @ w_query).reshape(T, PEER_HEADS, 2, PEER_D_HALF)
    scores = jnp.einsum("thpd,hpkd->thpk", q, sub_keys).astype(jnp.float32)
    s1, i1 = lax.top_k(scores[:, :, 0], PEER_TOPK)
    s2, i2 = lax.top_k(scores[:, :, 1], PEER_TOPK)
    cand = (s1[..., :, None] + s2[..., None, :]).reshape(T, PEER_HEADS, PEER_TOPK * PEER_TOPK)
    s, ci = lax.top_k(cand, PEER_TOPK)
    e_idx = (jnp.take_along_axis(i1, ci // PEER_TOPK, axis=-1) * PEER_N_KEYS
             + jnp.take_along_axis(i2, ci % PEER_TOPK, axis=-1))
    gate = jax.nn.softmax(s, axis=-1).astype(n.dtype)
    nb = T // PEER_TOKEN_BLOCK
    hk = PEER_HEADS * PEER_TOPK
    x_blk = xt.reshape(nb, PEER_TOKEN_BLOCK, D)
    i_blk = e_idx.reshape(nb, PEER_TOKEN_BLOCK, hk)
    g_blk = gate.reshape(nb, PEER_TOKEN_BLOCK, hk)

    def apply_experts(args):
        xb, ib, gb = args
        u = expert_u[ib]
        a = jnp.einsum("td,tkd->tk", xb, u)
        h = jax.nn.gelu(a) * gb
        return jnp.einsum("tk,tkd->td", h, expert_v[ib])

    y = lax.map(apply_experts, (x_blk, i_blk, g_blk))
    return y.reshape(B, S, D)


def setup_inputs(seed: int = 0) -> dict:
    key = jax.random.key(seed)
    ks = jax.random.split(key, 24)
    f32 = jnp.float32
    L, D = DEPTH, D_MODEL

    def nrm(k, shape, scale):
        return jax.random.normal(k, shape, f32) * scale

    return {
        "x": nrm(ks[0], (BATCH, SEQ, D), 1.0),
        "c": nrm(ks[1], (BATCH, D), 1.0),
        "w_ada": nrm(ks[2], (L, D, 6 * D), 0.5 * D ** -0.5),
        "b_ada": nrm(ks[3], (L, 6 * D), 0.01),
        "g_norm1": 1.0 + nrm(ks[4], (L, D), 0.02),
        "w_in": nrm(ks[5], (L, D, IN_WIDTH), D ** -0.5),
        "conv_dw_w": nrm(ks[6], (L, CONV_WIDTH, CONV_CH), CONV_WIDTH ** -0.5),
        "conv_dw_b": nrm(ks[7], (L, CONV_CH), 0.01),
        "conv_ln_g": 1.0 + nrm(ks[8], (L, CONV_CH), 0.02),
        "conv_ln_b": nrm(ks[9], (L, CONV_CH), 0.01),
        "w_conv_out": nrm(ks[10], (L, CONV_CH, D), CONV_CH ** -0.5),
        "sgu_ln_g": 1.0 + nrm(ks[11], (L, SGU_CH), 0.02),
        "sgu_ln_b": nrm(ks[12], (L, SGU_CH), 0.01),
        "w_spatial": nrm(ks[13], (L, SGU_GROUPS, SGU_BLOCK, SGU_BLOCK), SGU_BLOCK ** -0.5),
        "b_spatial": 1.0 + nrm(ks[14], (L, SGU_GROUPS, SGU_BLOCK), 0.02),
        "w_sgu_out": nrm(ks[15], (L, SGU_CH, D), SGU_CH ** -0.5),
        "w_out": nrm(ks[16], (L, D, D), D ** -0.5),
        "g_norm2": 1.0 + nrm(ks[17], (L, D), 0.02),
        "w_query": nrm(ks[18], (L, D, PEER_HEADS * PEER_D_KEY), D ** -0.5),
        "sub_keys": nrm(ks[19], (L, PEER_HEADS, 2, PEER_N_KEYS, PEER_D_HALF), PEER_D_HALF ** -0.5),
        "expert_u": nrm(ks[20], (L, PEER_N_EXPERTS, D), D ** -0.5),
        "expert_v": nrm(ks[21], (L, PEER_N_EXPERTS, D), 1.0),
        "g_final": 1.0 + nrm(ks[22], (D,), 0.02),
    }


def reference(x, c, w_ada, b_ada, g_norm1, w_in, conv_dw_w, conv_dw_b, conv_ln_g, conv_ln_b,
              w_conv_out, sgu_ln_g, sgu_ln_b, w_spatial, b_spatial, w_sgu_out, w_out,
              g_norm2, w_query, sub_keys, expert_u, expert_v, g_final):
    splits = [CONV_CH, 2 * CONV_CH, 2 * CONV_CH + SGU_CH, 2 * CONV_CH + 2 * SGU_CH,
              2 * CONV_CH + 2 * SGU_CH + D_MODEL]
    c_act = jax.nn.silu(c)
    h = x
    for l in range(DEPTH):
        mod = c_act @ w_ada[l] + b_ada[l]
        shift1, scale1, gate1, shift2, scale2, gate2 = jnp.split(mod, 6, axis=-1)

        n = modulate(rms_norm(h, g_norm1[l]), shift1, scale1)
        p = n @ w_in[l]
        a_val, a_gate, u, v, gate_a, gate_b = jnp.split(p, splits, axis=-1)
        y_a = conv_branch(a_val, a_gate, conv_dw_w[l], conv_dw_b[l], conv_ln_g[l], conv_ln_b[l],
                          w_conv_out[l])
        y_b = sgu_branch(u, v, sgu_ln_g[l], sgu_ln_b[l], w_spatial[l], b_spatial[l], w_sgu_out[l])
        merged = jax.nn.sigmoid(gate_a) * y_a + jax.nn.sigmoid(gate_b) * y_b
        h = h + gate1[:, None, :] * (merged @ w_out[l])

        n2 = modulate(rms_norm(h, g_norm2[l]), shift2, scale2)
        h = h + gate2[:, None, :] * peer(n2, w_query[l], sub_keys[l], expert_u[l], expert_v[l])
    return rms_norm(h, g_final)
```

```python
import functools

import jax
import jax.numpy as jnp
from jax import lax
from jax.experimental import pallas as pl
from jax.experimental.pallas import tpu as pltpu

F32 = jnp.float32
BF16 = jnp.bfloat16

D = 1024
EPS = 1e-6
CONV_W = 31
HALO = 32
SGU_G = 8
SGU_P = 128
CHUNK = 64
HEADS = 8
NKEYS = 128
TOPK = 16
NEXP = NKEYS * NKEYS
LANES = 128
SUBLANES = 8
VMEM_LIMIT = 56 * 1024 * 1024

NEG_INF = float("-inf")


def _sigmoid(x):
    return 1.0 / (1.0 + jnp.exp(-x))


def _gelu_tanh(x):
    return 0.5 * x * (1.0 + jnp.tanh(0.7978845608028654 * (x + 0.044715 * (x * x * x))))


def _const_spec(shape):
    nd = len(shape)
    return pl.BlockSpec(shape, lambda *_: (0,) * nd, pipeline_mode=pl.Buffered(1))


def _adaln_kernel(c_ref, w_ref, b_ref, o_ref):
    c = c_ref[...]
    ca = c * _sigmoid(c)
    o_ref[...] = jnp.dot(ca, w_ref[...], precision=lax.Precision.HIGHEST,
                         preferred_element_type=F32) + b_ref[...]


def _adaln(c, w, b, tn=1536):
    bsz, n = c.shape[0], w.shape[1]
    return pl.pallas_call(
        _adaln_kernel,
        grid=(n // tn,),
        in_specs=[pl.BlockSpec((bsz, D), lambda j: (0, 0)),
                  pl.BlockSpec((D, tn), lambda j: (0, j)),
                  pl.BlockSpec((1, tn), lambda j: (0, j))],
        out_specs=pl.BlockSpec((bsz, tn), lambda j: (0, j)),
        out_shape=jax.ShapeDtypeStruct((bsz, n), F32),
        compiler_params=pltpu.CompilerParams(dimension_semantics=("arbitrary",),
                                             vmem_limit_bytes=VMEM_LIMIT),
        name="adaln",
    )(c, w, b.reshape(1, n))


def _layer_norm_rows(x, g, b):
    mu = jnp.mean(x, axis=-1, keepdims=True)
    xc = x - mu
    var = jnp.mean(xc * xc, axis=-1, keepdims=True)
    return xc * lax.rsqrt(var + EPS) * g + b


def _sub1_kernel(x_ref, mod_ref, g1_ref, win_ref, dww_ref, dwb_ref, clg_ref, clb_ref, wco_ref,
                 slg_ref, slb_ref, wsp_ref, bsp_ref, wso_ref, wout_ref, h_ref,
                 zbuf, abuf, vbuf, mbuf, *, conv_rows):
    ts = x_ref.shape[1]
    s = pl.program_id(1)

    @pl.when(s == 0)
    def _():
        zbuf[0:HALO, :] = jnp.zeros((HALO, D), F32)

    x = x_ref[0]
    shift1 = mod_ref[0, 0:1, :]
    scale1 = mod_ref[0, 1:2, :]
    gate1 = mod_ref[0, 2:3, :]
    rn = x * lax.rsqrt(jnp.mean(x * x, axis=-1, keepdims=True) + EPS) * g1_ref[...]
    n = (rn * (1.0 + scale1) + shift1).astype(BF16)

    def proj(c):
        return jnp.dot(n, win_ref[:, c * D:(c + 1) * D], preferred_element_type=F32)

    zbuf[HALO:HALO + ts, :] = proj(0) * _sigmoid(proj(1))

    for ci in range(ts // conv_rows):
        r0 = ci * conv_rows
        acc = jnp.broadcast_to(dwb_ref[...], (conv_rows, D))
        for k in range(CONV_W):
            off = HALO - (CONV_W - 1) + k
            acc = acc + zbuf[r0 + off:r0 + off + conv_rows, :] * dww_ref[k:k + 1, :]
        a = _layer_norm_rows(acc, clg_ref[...], clb_ref[...])
        abuf[r0:r0 + conv_rows, :] = (a * _sigmoid(a)).astype(BF16)
    zbuf[0:HALO, :] = zbuf[ts:ts + HALO, :]
    y_a = jnp.dot(abuf[...], wco_ref[...], preferred_element_type=F32)
    merged = _sigmoid(proj(4)) * y_a

    vbuf[...] = _layer_norm_rows(proj(3), slg_ref[...], slb_ref[...]).astype(BF16)
    pos_chunk_i = lax.broadcasted_iota(jnp.int32, (SGU_P, SGU_P), 0) // CHUNK
    pos_chunk_j = lax.broadcasted_iota(jnp.int32, (SGU_P, SGU_P), 1) // CHUNK
    keep = pos_chunk_j <= pos_chunk_i
    for g in range(SGU_G):
        wm = jnp.where(keep, wsp_ref[g], 0.0).astype(BF16)
        cols = slice(g * SGU_P, (g + 1) * SGU_P)
        for blk in range(ts // SGU_P):
            rows = slice(blk * SGU_P, (blk + 1) * SGU_P)
            mixed = jnp.dot(wm, vbuf[rows, cols], preferred_element_type=F32)
            mbuf[rows, cols] = mixed + bsp_ref[:, cols]
    gated = (proj(2) * mbuf[...]).astype(BF16)
    y_b = jnp.dot(gated, wso_ref[...], preferred_element_type=F32)
    merged = (merged + _sigmoid(proj(5)) * y_b).astype(BF16)

    o1 = jnp.dot(merged, wout_ref[...], preferred_element_type=F32)
    h_ref[0] = x + gate1 * o1


def _sub1(x, mod3, g1, win, dww, dwb, clg, clb, wco, slg, slb, wsp, bsp_full, wso, wout,
          ts=256, conv_rows=32):
    bsz, seq, _ = x.shape
    row = lambda a: a.reshape(1, D)
    kern = functools.partial(_sub1_kernel, conv_rows=conv_rows)
    return pl.pallas_call(
        kern,
        grid=(bsz, seq // ts),
        in_specs=[pl.BlockSpec((1, ts, D), lambda b, s: (b, s, 0)),
                  pl.BlockSpec((1, 6, D), lambda b, s: (b, 0, 0)),
                  _const_spec((1, D)),
                  _const_spec((D, 6 * D)),
                  _const_spec((CONV_W, D)),
                  _const_spec((1, D)),
                  _const_spec((1, D)),
                  _const_spec((1, D)),
                  _const_spec((D, D)),
                  _const_spec((1, D)),
                  _const_spec((1, D)),
                  _const_spec((SGU_G, SGU_P, SGU_P)),
                  _const_spec((SGU_P, D)),
                  _const_spec((D, D)),
                  _const_spec((D, D))],
        out_specs=pl.BlockSpec((1, ts, D), lambda b, s: (b, s, 0)),
        out_shape=jax.ShapeDtypeStruct((bsz, seq, D), F32),
        scratch_shapes=[pltpu.VMEM((HALO + ts, D), F32),
                        pltpu.VMEM((ts, D), BF16),
                        pltpu.VMEM((ts, D), BF16),
                        pltpu.VMEM((ts, D), F32)],
        compiler_params=pltpu.CompilerParams(dimension_semantics=("arbitrary", "arbitrary"),
                                             vmem_limit_bytes=VMEM_LIMIT),
        name="sub1",
    )(x, mod3, row(g1), win, dww, row(dwb), row(clg), row(clb), wco, row(slg), row(slb),
      wsp, bsp_full, wso, wout)


def _tree(op, xs):
    xs = list(xs)
    while len(xs) > 1:
        nxt = [op(xs[i], xs[i + 1]) for i in range(0, len(xs) - 1, 2)]
        if len(xs) % 2:
            nxt.append(xs[-1])
        xs = nxt
    return xs[0]


def _extract_topk(work, rank, vals, n):
    def key_rows(k):
        return pl.ds(k * SUBLANES, SUBLANES)

    for k in range(n):
        rank[key_rows(k), :] = jnp.full((SUBLANES, LANES), float(TOPK), F32)

    def round_body(r, carry):
        s = [work[key_rows(k), :] for k in range(n)]
        m = _tree(jnp.maximum, s)
        first = _tree(jnp.minimum, [jnp.where(s[k] == m, float(k), float(n)) for k in range(n)])
        vals[pl.ds(pl.multiple_of(r * SUBLANES, SUBLANES), SUBLANES), :] = m
        rf = r.astype(F32)
        for k in range(n):
            hit = first == float(k)
            work[key_rows(k), :] = jnp.where(hit, NEG_INF, s[k])
            rank[key_rows(k), :] = jnp.where(hit, rf, rank[key_rows(k), :])
        return carry

    lax.fori_loop(0, TOPK, round_body, 0)


def _route_kernel(h_ref, mod_ref, g2_ref, wqt_ref, kbig_ref,
                  n2t_ref, c_ref, p1_ref, r2_ref, p2_ref,
                  s1_scr, s2_scr, w1, w2, rk1, rk2, v1, v2, o_c, o_p1, o_p2):
    tr = h_ref.shape[0]
    h = h_ref[...]
    shift2 = mod_ref[0, 3:4, :]
    scale2 = mod_ref[0, 4:5, :]
    rn = h * lax.rsqrt(jnp.mean(h * h, axis=-1, keepdims=True) + EPS) * g2_ref[...]
    n2 = rn * (1.0 + scale2) + shift2
    n2t = n2.T.astype(BF16)
    n2t_ref[...] = n2t
    qt = jnp.dot(wqt_ref[...], n2t, preferred_element_type=F32).astype(BF16)
    half = HEADS * NKEYS
    s1_scr[...] = jnp.dot(kbig_ref[0], qt[0:half], preferred_element_type=F32)
    s2_scr[...] = jnp.dot(kbig_ref[1], qt[half:2 * half], preferred_element_type=F32)

    cands = [(a, b) for a in range(TOPK) for b in range(TOPK) if (a + 1) * (b + 1) <= TOPK]

    def block_body(tb, carry):
        lane = pl.ds(pl.multiple_of(tb * LANES, LANES), LANES)
        w1[...] = s1_scr[:, lane]
        w2[...] = s2_scr[:, lane]
        _extract_topk(w1, rk1, v1, NKEYS)
        _extract_topk(w2, rk2, v2, NKEYS)

        def rows(k):
            return pl.ds(k * SUBLANES, SUBLANES)

        av = [v1[rows(a), :] for a in range(TOPK)]
        bv = [v2[rows(b), :] for b in range(TOPK)]
        cv = [av[a] + bv[b] for (a, b) in cands]
        nc = len(cv)
        z = None
        m0 = None
        for r in range(TOPK):
            m = _tree(jnp.maximum, cv)
            first = _tree(jnp.minimum,
                          [jnp.where(cv[j] == m, float(j), float(nc)) for j in range(nc)])
            cv = [jnp.where(first == float(j), NEG_INF, cv[j]) for j in range(nc)]
            if r == 0:
                m0 = m
                z = jnp.ones_like(m)
            else:
                z = z + jnp.exp(m - m0)
        inv_z = 1.0 / z
        cnt = []
        for a in range(TOPK):
            sel = [jnp.where(cv[j] == NEG_INF, 1.0, 0.0) for j, (ca, _) in enumerate(cands) if ca == a]
            cnt.append(_tree(jnp.add, sel))
        for k in range(NKEYS):
            rk = rk1[rows(k), :]
            ck = jnp.zeros((SUBLANES, LANES), F32)
            for a in range(TOPK):
                ck = jnp.where(rk == float(a), cnt[a], ck)
            o_c[rows(k), :] = ck
            o_p1[rows(k), :] = jnp.exp(s1_scr[rows(k), lane] - av[0]) * inv_z
            o_p2[rows(k), :] = jnp.exp(s2_scr[rows(k), lane] - bv[0])
        for hd in range(HEADS):
            sel_rows = pl.ds(hd, NKEYS, stride=HEADS)
            c_ref[hd, :, lane] = o_c[sel_rows, :]
            p1_ref[hd, :, lane] = o_p1[sel_rows, :]
            r2_ref[hd, :, lane] = rk2[sel_rows, :].astype(BF16)
            p2_ref[hd, :, lane] = o_p2[sel_rows, :].astype(BF16)
        return carry

    lax.fori_loop(0, tr // LANES, block_body, 0)


def _route(h2d, mod3, g2, wqt, kbig, seq, tr=512):
    t = h2d.shape[0]
    tbl = lambda dt: jax.ShapeDtypeStruct((HEADS, NKEYS, t), dt)
    tbl_spec = pl.BlockSpec((HEADS, NKEYS, tr), lambda i: (0, 0, i))
    scr = lambda n: pltpu.VMEM((n * SUBLANES, LANES), F32)
    return pl.pallas_call(
        _route_kernel,
        grid=(t // tr,),
        in_specs=[pl.BlockSpec((tr, D), lambda i: (i, 0)),
                  pl.BlockSpec((1, 6, D), lambda i: ((i * tr) // seq, 0, 0)),
                  _const_spec((1, D)),
                  _const_spec((2 * HEADS * NKEYS, D)),
                  _const_spec((2, HEADS * NKEYS, HEADS * NKEYS))],
        out_specs=[pl.BlockSpec((D, tr), lambda i: (0, i)),
                   tbl_spec, tbl_spec, tbl_spec, tbl_spec],
        out_shape=[jax.ShapeDtypeStruct((D, t), BF16), tbl(F32), tbl(F32), tbl(BF16), tbl(BF16)],
        scratch_shapes=[pltpu.VMEM((HEADS * NKEYS, tr), F32),
                        pltpu.VMEM((HEADS * NKEYS, tr), F32),
                        scr(NKEYS), scr(NKEYS), scr(NKEYS), scr(NKEYS), scr(TOPK), scr(TOPK),
                        scr(NKEYS), scr(NKEYS), scr(NKEYS)],
        compiler_params=pltpu.CompilerParams(dimension_semantics=("arbitrary",),
                                             vmem_limit_bytes=VMEM_LIMIT),
        name="route",
    )(h2d, mod3, g2.reshape(1, D), wqt, kbig)


def _experts_kernel(n2t_ref, c_ref, p1_ref, r2_ref, p2_ref, u_ref, vt_ref, h_ref, mod_ref, gf_ref,
                    o_ref, acc, a_scr, h_scr):
    te = u_ref.shape[0]
    tm = n2t_ref.shape[1]
    j = pl.program_id(1)
    pack = 2 * SUBLANES

    @pl.when(j == 0)
    def _():
        acc[...] = jnp.zeros_like(acc)

    a_scr[...] = jnp.dot(u_ref[...], n2t_ref[...], preferred_element_type=F32)

    def block_body(tb, carry):
        lane = pl.ds(pl.multiple_of(tb * LANES, LANES), LANES)
        irows = pl.ds(pl.multiple_of(j * (te // NKEYS), SUBLANES), SUBLANES)
        c8 = [c_ref[hd, irows, lane] for hd in range(HEADS)]
        p8 = [p1_ref[hd, irows, lane] for hd in range(HEADS)]
        for il in range(te // NKEYS):
            cb = [jnp.broadcast_to(c8[hd][il:il + 1, :], (pack, LANES)).astype(BF16)
                  for hd in range(HEADS)]
            pb = [jnp.broadcast_to(p8[hd][il:il + 1, :], (pack, LANES)).astype(BF16)
                  for hd in range(HEADS)]
            for jb in range(NKEYS // pack):
                rows = pl.ds(il * NKEYS + jb * pack, pack)
                jrows = pl.ds(jb * pack, pack)
                g = _gelu_tanh(a_scr[rows, lane]).astype(BF16)
                w = jnp.zeros((pack, LANES), BF16)
                for hd in range(HEADS):
                    r2 = r2_ref[hd, jrows, lane]
                    p2 = p2_ref[hd, jrows, lane]
                    w = w + jnp.where(r2 < cb[hd], p2 * pb[hd], jnp.zeros_like(p2))
                h_scr[rows, lane] = g * w
        return carry

    lax.fori_loop(0, tm // LANES, block_body, 0)
    acc[...] += jnp.dot(vt_ref[...], h_scr[...], preferred_element_type=F32)

    @pl.when(j == pl.num_programs(1) - 1)
    def _():
        gate2 = mod_ref[0, 5:6, :]
        h2 = h_ref[...] + gate2 * acc[...].T
        o_ref[...] = (h2 * lax.rsqrt(jnp.mean(h2 * h2, axis=-1, keepdims=True) + EPS)
                      * gf_ref[...])


def _experts(n2t, c_t, p1_t, r2_t, p2_t, u_bf, vt_bf, h2d, mod3, gf, seq, tm=512, te=1024):
    t = h2d.shape[0]
    assert te == SUBLANES * NKEYS, "one expert block is one sublane group of first-key rows"
    tblf = pl.BlockSpec((HEADS, NKEYS, tm), lambda i, j: (0, 0, i))
    return pl.pallas_call(
        _experts_kernel,
        grid=(t // tm, NEXP // te),
        in_specs=[pl.BlockSpec((D, tm), lambda i, j: (0, i)),
                  tblf, tblf, tblf, tblf,
                  pl.BlockSpec((te, D), lambda i, j: (j, 0)),
                  pl.BlockSpec((D, te), lambda i, j: (0, j)),
                  pl.BlockSpec((tm, D), lambda i, j: (i, 0)),
                  pl.BlockSpec((1, 6, D), lambda i, j: ((i * tm) // seq, 0, 0)),
                  _const_spec((1, D))],
        out_specs=pl.BlockSpec((tm, D), lambda i, j: (i, 0)),
        out_shape=jax.ShapeDtypeStruct((t, D), F32),
        scratch_shapes=[pltpu.VMEM((D, tm), F32),
                        pltpu.VMEM((te, tm), F32),
                        pltpu.VMEM((te, tm), BF16)],
        compiler_params=pltpu.CompilerParams(dimension_semantics=("arbitrary", "arbitrary"),
                                             vmem_limit_bytes=VMEM_LIMIT),
        name="experts",
    )(n2t, c_t, p1_t, r2_t, p2_t, u_bf, vt_bf, h2d, mod3, gf.reshape(1, D))


def _expand_keys(sub_keys):
    h, _, k, dh = sub_keys.shape
    eye = jnp.eye(h, dtype=sub_keys.dtype)
    big = jnp.einsum("hpkd,hg->pkhgd", sub_keys, eye)
    return big.reshape(2, k * h, h * dh)


def kernel(x, c, w_ada, b_ada, g_norm1, w_in, conv_dw_w, conv_dw_b, conv_ln_g, conv_ln_b,
           w_conv_out, sgu_ln_g, sgu_ln_b, w_spatial, b_spatial, w_sgu_out, w_out, g_norm2,
           w_query, sub_keys, expert_u, expert_v, g_final):
    bsz, seq, _ = x.shape
    depth = w_ada.shape[0]
    h = x
    for l in range(depth):
        mod3 = _adaln(c, w_ada[l], b_ada[l]).reshape(bsz, 6, D)
        bsp_full = jnp.repeat(b_spatial[l].T, SGU_P, axis=1)
        h = _sub1(h, mod3, g_norm1[l], w_in[l].astype(BF16), conv_dw_w[l], conv_dw_b[l],
                  conv_ln_g[l], conv_ln_b[l], w_conv_out[l].astype(BF16), sgu_ln_g[l],
                  sgu_ln_b[l], w_spatial[l], bsp_full, w_sgu_out[l].astype(BF16),
                  w_out[l].astype(BF16))
        wqt = (w_query[l].reshape(D, HEADS, 2, NKEYS).transpose(2, 1, 3, 0)
               .reshape(2 * HEADS * NKEYS, D).astype(BF16))
        kbig = _expand_keys(sub_keys[l]).astype(BF16)
        h2d = h.reshape(bsz * seq, D)
        n2t, c_t, p1_t, r2_t, p2_t = _route(h2d, mod3, g_norm2[l], wqt, kbig, seq)
        last = l == depth - 1
        gf = g_final if last else jnp.ones_like(g_final)
        out = _experts(n2t, c_t, p1_t, r2_t, p2_t, expert_u[l].astype(BF16),
                       expert_v[l].T.astype(BF16), h2d, mod3, gf, seq)
        h = out.reshape(bsz, seq, D)
    return h
```

```python
import functools

import jax
import jax.numpy as jnp
from jax import lax
from jax.experimental import pallas as pl
from jax.experimental.pallas import tpu as pltpu

F32 = jnp.float32
BF16 = jnp.bfloat16

D = 1024
EPS = 1e-6
CONV_W = 31
HALO = 32
SGU_G = 8
SGU_P = 128
CHUNK = 64
HEADS = 8
NKEYS = 128
TOPK = 16
NEXP = NKEYS * NKEYS
LANES = 128
SUBLANES = 8
VMEM_LIMIT = 56 * 1024 * 1024

NEG_INF = float("-inf")


def _sigmoid(x):
    return 1.0 / (1.0 + jnp.exp(-x))


def _gelu_tanh(x):
    k = 0.7978845608028654
    t = jnp.tanh(x * (k + (k * 0.044715) * (x * x)))
    return x * (0.5 + 0.5 * t)


def _const_spec(shape):
    nd = len(shape)
    return pl.BlockSpec(shape, lambda *_: (0,) * nd, pipeline_mode=pl.Buffered(1))


def _adaln_kernel(c_ref, w_ref, b_ref, o_ref):
    c = c_ref[...]
    ca = c * _sigmoid(c)
    o_ref[...] = jnp.dot(ca, w_ref[...], precision=lax.Precision.HIGHEST,
                         preferred_element_type=F32) + b_ref[...]


def _adaln(c, w, b, tn=1536):
    bsz, n = c.shape[0], w.shape[1]
    return pl.pallas_call(
        _adaln_kernel,
        grid=(n // tn,),
        in_specs=[pl.BlockSpec((bsz, D), lambda j: (0, 0)),
                  pl.BlockSpec((D, tn), lambda j: (0, j)),
                  pl.BlockSpec((1, tn), lambda j: (0, j))],
        out_specs=pl.BlockSpec((bsz, tn), lambda j: (0, j)),
        out_shape=jax.ShapeDtypeStruct((bsz, n), F32),
        compiler_params=pltpu.CompilerParams(dimension_semantics=("arbitrary",),
                                             vmem_limit_bytes=VMEM_LIMIT),
        name="adaln",
    )(c, w, b.reshape(1, n))


def _layer_norm_rows(x, g, b):
    mu = jnp.mean(x, axis=-1, keepdims=True)
    xc = x - mu
    var = jnp.mean(xc * xc, axis=-1, keepdims=True)
    return xc * lax.rsqrt(var + EPS) * g + b


def _sub1_kernel(x_ref, mod_ref, g1_ref, win_ref, dww_ref, dwb_ref, clg_ref, clb_ref, wco_ref,
                 slg_ref, slb_ref, wsp_ref, bsp_ref, wso_ref, wout_ref, h_ref,
                 zbuf, abuf, vbuf, mbuf, *, conv_rows):
    ts = x_ref.shape[1]
    s = pl.program_id(1)

    @pl.when(s == 0)
    def _():
        zbuf[0:HALO, :] = jnp.zeros((HALO, D), F32)

    x = x_ref[0]
    shift1 = mod_ref[0, 0:1, :]
    scale1 = mod_ref[0, 1:2, :]
    gate1 = mod_ref[0, 2:3, :]
    rn = x * lax.rsqrt(jnp.mean(x * x, axis=-1, keepdims=True) + EPS) * g1_ref[...]
    n = (rn * (1.0 + scale1) + shift1).astype(BF16)

    def proj(c):
        return jnp.dot(n, win_ref[:, c * D:(c + 1) * D], preferred_element_type=F32)

    zbuf[HALO:HALO + ts, :] = proj(0) * _sigmoid(proj(1))

    for ci in range(ts // conv_rows):
        r0 = ci * conv_rows
        acc = jnp.broadcast_to(dwb_ref[...], (conv_rows, D))
        for k in range(CONV_W):
            off = HALO - (CONV_W - 1) + k
            acc = acc + zbuf[r0 + off:r0 + off + conv_rows, :] * dww_ref[k:k + 1, :]
        a = _layer_norm_rows(acc, clg_ref[...], clb_ref[...])
        abuf[r0:r0 + conv_rows, :] = (a * _sigmoid(a)).astype(BF16)
    zbuf[0:HALO, :] = zbuf[ts:ts + HALO, :]
    y_a = jnp.dot(abuf[...], wco_ref[...], preferred_element_type=F32)
    merged = _sigmoid(proj(4)) * y_a

    vbuf[...] = _layer_norm_rows(proj(3), slg_ref[...], slb_ref[...]).astype(BF16)
    pos_chunk_i = lax.broadcasted_iota(jnp.int32, (SGU_P, SGU_P), 0) // CHUNK
    pos_chunk_j = lax.broadcasted_iota(jnp.int32, (SGU_P, SGU_P), 1) // CHUNK
    keep = pos_chunk_j <= pos_chunk_i
    for g in range(SGU_G):
        wm = jnp.where(keep, wsp_ref[g], 0.0).astype(BF16)
        cols = slice(g * SGU_P, (g + 1) * SGU_P)
        for blk in range(ts // SGU_P):
            rows = slice(blk * SGU_P, (blk + 1) * SGU_P)
            mixed = jnp.dot(wm, vbuf[rows, cols], preferred_element_type=F32)
            mbuf[rows, cols] = mixed + bsp_ref[:, cols]
    gated = (proj(2) * mbuf[...]).astype(BF16)
    y_b = jnp.dot(gated, wso_ref[...], preferred_element_type=F32)
    merged = (merged + _sigmoid(proj(5)) * y_b).astype(BF16)

    o1 = jnp.dot(merged, wout_ref[...], preferred_element_type=F32)
    h_ref[0] = x + gate1 * o1


def _sub1(x, mod3, g1, win, dww, dwb, clg, clb, wco, slg, slb, wsp, bsp_full, wso, wout,
          ts=256, conv_rows=32):
    bsz, seq, _ = x.shape
    row = lambda a: a.reshape(1, D)
    kern = functools.partial(_sub1_kernel, conv_rows=conv_rows)
    return pl.pallas_call(
        kern,
        grid=(bsz, seq // ts),
        in_specs=[pl.BlockSpec((1, ts, D), lambda b, s: (b, s, 0)),
                  pl.BlockSpec((1, 6, D), lambda b, s: (b, 0, 0)),
                  _const_spec((1, D)),
                  _const_spec((D, 6 * D)),
                  _const_spec((CONV_W, D)),
                  _const_spec((1, D)),
                  _const_spec((1, D)),
                  _const_spec((1, D)),
                  _const_spec((D, D)),
                  _const_spec((1, D)),
                  _const_spec((1, D)),
                  _const_spec((SGU_G, SGU_P, SGU_P)),
                  _const_spec((SGU_P, D)),
                  _const_spec((D, D)),
                  _const_spec((D, D))],
        out_specs=pl.BlockSpec((1, ts, D), lambda b, s: (b, s, 0)),
        out_shape=jax.ShapeDtypeStruct((bsz, seq, D), F32),
        scratch_shapes=[pltpu.VMEM((HALO + ts, D), F32),
                        pltpu.VMEM((ts, D), BF16),
                        pltpu.VMEM((ts, D), BF16),
                        pltpu.VMEM((ts, D), F32)],
        compiler_params=pltpu.CompilerParams(dimension_semantics=("arbitrary", "arbitrary"),
                                             vmem_limit_bytes=VMEM_LIMIT),
        name="sub1",
    )(x, mod3, row(g1), win, dww, row(dwb), row(clg), row(clb), wco, row(slg), row(slb),
      wsp, bsp_full, wso, wout)


def _tree(op, xs):
    xs = list(xs)
    while len(xs) > 1:
        nxt = [op(xs[i], xs[i + 1]) for i in range(0, len(xs) - 1, 2)]
        if len(xs) % 2:
            nxt.append(xs[-1])
        xs = nxt
    return xs[0]


def _extract_topk(work, rank, vals, n):
    def key_rows(k):
        return pl.ds(k * SUBLANES, SUBLANES)

    for k in range(n):
        rank[key_rows(k), :] = jnp.full((SUBLANES, LANES), float(TOPK), F32)

    def round_body(r, carry):
        s = [work[key_rows(k), :] for k in range(n)]
        m = _tree(jnp.maximum, s)
        first = _tree(jnp.minimum, [jnp.where(s[k] == m, float(k), float(n)) for k in range(n)])
        vals[pl.ds(pl.multiple_of(r * SUBLANES, SUBLANES), SUBLANES), :] = m
        rf = r.astype(F32)
        for k in range(n):
            hit = first == float(k)
            work[key_rows(k), :] = jnp.where(hit, NEG_INF, s[k])
            rank[key_rows(k), :] = jnp.where(hit, rf, rank[key_rows(k), :])
        return carry

    lax.fori_loop(0, TOPK, round_body, 0)


def _route_kernel(h_ref, mod_ref, g2_ref, wqt_ref, kbig_ref,
                  n2t_ref, c_ref, p1_ref, r2_ref, p2_ref,
                  s1_scr, s2_scr, w1, w2, rk1, rk2, v1, v2, o_c, o_p1, o_p2):
    tr = h_ref.shape[0]
    h = h_ref[...]
    shift2 = mod_ref[0, 3:4, :]
    scale2 = mod_ref[0, 4:5, :]
    rn = h * lax.rsqrt(jnp.mean(h * h, axis=-1, keepdims=True) + EPS) * g2_ref[...]
    n2 = rn * (1.0 + scale2) + shift2
    n2t = n2.T.astype(BF16)
    n2t_ref[...] = n2t
    qt = jnp.dot(wqt_ref[...], n2t, preferred_element_type=F32).astype(BF16)
    half = HEADS * NKEYS
    s1_scr[...] = jnp.dot(kbig_ref[0], qt[0:half], preferred_element_type=F32)
    s2_scr[...] = jnp.dot(kbig_ref[1], qt[half:2 * half], preferred_element_type=F32)

    cands = [(a, b) for a in range(TOPK) for b in range(TOPK) if (a + 1) * (b + 1) <= TOPK]

    def block_body(tb, carry):
        lane = pl.ds(pl.multiple_of(tb * LANES, LANES), LANES)
        w1[...] = s1_scr[:, lane]
        w2[...] = s2_scr[:, lane]
        _extract_topk(w1, rk1, v1, NKEYS)
        _extract_topk(w2, rk2, v2, NKEYS)

        def rows(k):
            return pl.ds(k * SUBLANES, SUBLANES)

        av = [v1[rows(a), :] for a in range(TOPK)]
        bv = [v2[rows(b), :] for b in range(TOPK)]
        cv = [av[a] + bv[b] for (a, b) in cands]
        nc = len(cv)
        z = None
        m0 = None
        for r in range(TOPK):
            m = _tree(jnp.maximum, cv)
            first = _tree(jnp.minimum,
                          [jnp.where(cv[j] == m, float(j), float(nc)) for j in range(nc)])
            cv = [jnp.where(first == float(j), NEG_INF, cv[j]) for j in range(nc)]
            if r == 0:
                m0 = m
                z = jnp.ones_like(m)
            else:
                z = z + jnp.exp(m - m0)
        inv_z = 1.0 / z
        cnt = []
        for a in range(TOPK):
            sel = [jnp.where(cv[j] == NEG_INF, 1.0, 0.0) for j, (ca, _) in enumerate(cands) if ca == a]
            cnt.append(_tree(jnp.add, sel))
        for k in range(NKEYS):
            rk = rk1[rows(k), :]
            ck = jnp.zeros((SUBLANES, LANES), F32)
            for a in range(TOPK):
                ck = jnp.where(rk == float(a), cnt[a], ck)
            o_c[rows(k), :] = ck
            o_p1[rows(k), :] = jnp.exp(s1_scr[rows(k), lane] - av[0]) * inv_z
            o_p2[rows(k), :] = jnp.exp(s2_scr[rows(k), lane] - bv[0])
        for hd in range(HEADS):
            sel_rows = pl.ds(hd, NKEYS, stride=HEADS)
            c_ref[hd, :, lane] = o_c[sel_rows, :]
            p1_ref[hd, :, lane] = o_p1[sel_rows, :]
            even = pl.ds(hd, NKEYS // 2, stride=2 * HEADS)
            odd = pl.ds(hd + HEADS, NKEYS // 2, stride=2 * HEADS)
            r2_ref[hd, :, lane] = pltpu.pack_elementwise([rk2[even, :], rk2[odd, :]],
                                                         packed_dtype=BF16)
            p2_ref[hd, :, lane] = pltpu.pack_elementwise([o_p2[even, :], o_p2[odd, :]],
                                                         packed_dtype=BF16)
        return carry

    lax.fori_loop(0, tr // LANES, block_body, 0)


def _route(h2d, mod3, g2, wqt, kbig, seq, tr=512):
    t = h2d.shape[0]
    tbl = jax.ShapeDtypeStruct((HEADS, NKEYS, t), F32)
    tbl_pair = jax.ShapeDtypeStruct((HEADS, NKEYS // 2, t), jnp.uint32)
    tbl_spec = pl.BlockSpec((HEADS, NKEYS, tr), lambda i: (0, 0, i))
    pair_spec = pl.BlockSpec((HEADS, NKEYS // 2, tr), lambda i: (0, 0, i))
    scr = lambda n: pltpu.VMEM((n * SUBLANES, LANES), F32)
    return pl.pallas_call(
        _route_kernel,
        grid=(t // tr,),
        in_specs=[pl.BlockSpec((tr, D), lambda i: (i, 0)),
                  pl.BlockSpec((1, 6, D), lambda i: ((i * tr) // seq, 0, 0)),
                  _const_spec((1, D)),
                  _const_spec((2 * HEADS * NKEYS, D)),
                  _const_spec((2, HEADS * NKEYS, HEADS * NKEYS))],
        out_specs=[pl.BlockSpec((D, tr), lambda i: (0, i)),
                   tbl_spec, tbl_spec, pair_spec, pair_spec],
        out_shape=[jax.ShapeDtypeStruct((D, t), BF16), tbl, tbl, tbl_pair, tbl_pair],
        scratch_shapes=[pltpu.VMEM((HEADS * NKEYS, tr), F32),
                        pltpu.VMEM((HEADS * NKEYS, tr), F32),
                        scr(NKEYS), scr(NKEYS), scr(NKEYS), scr(NKEYS), scr(TOPK), scr(TOPK),
                        scr(NKEYS), scr(NKEYS), scr(NKEYS)],
        compiler_params=pltpu.CompilerParams(dimension_semantics=("arbitrary",),
                                             vmem_limit_bytes=VMEM_LIMIT),
        name="route",
    )(h2d, mod3, g2.reshape(1, D), wqt, kbig)


EXPERT_HALF = SUBLANES * NKEYS
PACK = 2 * SUBLANES


def _routing_weights_times_gelu(tb, ils, a_ref, out_ref, c_ref, p1_ref, r2_ref, p2_ref):
    if True:
        lane = slice(tb * LANES, (tb + 1) * LANES)
        c8 = [c_ref[hd, :, lane] for hd in range(HEADS)]
        p8 = [p1_ref[hd, :, lane] for hd in range(HEADS)]
        for il in ils:
            cb = [jnp.broadcast_to(c8[hd][il:il + 1, :], (PACK, LANES)).astype(BF16)
                  for hd in range(HEADS)]
            pb = [jnp.broadcast_to(p8[hd][il:il + 1, :], (PACK, LANES)).astype(BF16)
                  for hd in range(HEADS)]
            for jb in range(NKEYS // PACK):
                rows = slice(il * NKEYS + jb * PACK, il * NKEYS + (jb + 1) * PACK)
                jrows = slice(jb * SUBLANES, (jb + 1) * SUBLANES)
                g = _gelu_tanh(a_ref[rows, lane]).astype(BF16)
                w = jnp.zeros((PACK, LANES), BF16)
                for hd in range(HEADS):
                    r2 = pltpu.bitcast(r2_ref[hd, jrows, lane], BF16)
                    p2 = pltpu.bitcast(p2_ref[hd, jrows, lane], BF16)
                    w = w + jnp.where(r2 < cb[hd], p2 * pb[hd], jnp.zeros_like(p2))
                out_ref[rows, lane] = g * w


def _experts_kernel(n2t_ref, c_odd_ref, p1_odd_ref, c_even_ref, p1_even_ref, r2_ref, p2_ref,
                    u_ref, vt_ref, h_ref, mod_ref, gf_ref, o_ref, acc, a0, a1, h0, h1):
    q = pl.program_id(1)
    nq = pl.num_programs(1) - 1

    @pl.when(q == 0)
    def _():
        acc[...] = jnp.zeros_like(acc)
        a1[...] = jnp.zeros_like(a1)
        h0[...] = jnp.zeros_like(h0)
        h1[...] = jnp.zeros_like(h1)

    tm = n2t_ref.shape[1]
    mxu_n = 2 * LANES
    row_pieces = 2
    piece_rows = EXPERT_HALF // row_pieces
    n_pieces = row_pieces * (tm // mxu_n)
    ew_per_tb = n_pieces // (tm // LANES)
    il_per_piece = SUBLANES // ew_per_tb

    def stage(e0, a_dst, a_src, h_dst, h_src, c_ref, p1_ref):
        pieces = [(r, n) for n in range(tm // mxu_n) for r in range(row_pieces)]
        for p, (r, n) in enumerate(pieces):
            rows = slice(r * piece_rows, (r + 1) * piece_rows)
            cols = slice(n * mxu_n, (n + 1) * mxu_n)
            a_dst[rows, cols] = jnp.dot(u_ref[e0 + r * piece_rows:e0 + (r + 1) * piece_rows, :],
                                        n2t_ref[:, cols], preferred_element_type=F32)
            ils = range((p % ew_per_tb) * il_per_piece, (p % ew_per_tb + 1) * il_per_piece)
            _routing_weights_times_gelu(p // ew_per_tb, ils, a_src, h_dst, c_ref, p1_ref,
                                        r2_ref, p2_ref)
            acc[rows, cols] += jnp.dot(vt_ref[rows, e0:e0 + EXPERT_HALF], h_src[:, cols],
                                       preferred_element_type=F32)

    stage(0, a0, a1, h1, h0, c_odd_ref, p1_odd_ref)
    stage(EXPERT_HALF, a1, a0, h0, h1, c_even_ref, p1_even_ref)

    @pl.when(q == nq)
    def _():
        gate2 = mod_ref[0, 5:6, :]
        h2 = h_ref[...] + gate2 * acc[...].T
        o_ref[...] = (h2 * lax.rsqrt(jnp.mean(h2 * h2, axis=-1, keepdims=True) + EPS)
                      * gf_ref[...])


def _experts(n2t, c_t, p1_t, r2_t, p2_t, u_bf, vt_bf, h2d, mod3, gf, seq, tm=512):
    t = h2d.shape[0]
    te = 2 * EXPERT_HALF
    nq = NEXP // te
    last_group = NKEYS // SUBLANES - 1
    odd = pl.BlockSpec((HEADS, SUBLANES, tm),
                       lambda i, q: (0, jnp.clip(2 * q - 1, 0, last_group), i))
    even = pl.BlockSpec((HEADS, SUBLANES, tm), lambda i, q: (0, jnp.minimum(2 * q, last_group), i))
    pair = pl.BlockSpec((HEADS, NKEYS // 2, tm), lambda i, q: (0, 0, i))
    return pl.pallas_call(
        _experts_kernel,
        grid=(t // tm, nq + 1),
        in_specs=[pl.BlockSpec((D, tm), lambda i, q: (0, i)),
                  odd, odd, even, even, pair, pair,
                  pl.BlockSpec((te, D), lambda i, q: (jnp.minimum(q, nq - 1), 0)),
                  pl.BlockSpec((D, te), lambda i, q: (0, jnp.maximum(q - 1, 0))),
                  pl.BlockSpec((tm, D), lambda i, q: (i, 0)),
                  pl.BlockSpec((1, 6, D), lambda i, q: ((i * tm) // seq, 0, 0)),
                  _const_spec((1, D))],
        out_specs=pl.BlockSpec((tm, D), lambda i, q: (i, 0)),
        out_shape=jax.ShapeDtypeStruct((t, D), F32),
        scratch_shapes=[pltpu.VMEM((D, tm), F32),
                        pltpu.VMEM((EXPERT_HALF, tm), F32),
                        pltpu.VMEM((EXPERT_HALF, tm), F32),
                        pltpu.VMEM((EXPERT_HALF, tm), BF16),
                        pltpu.VMEM((EXPERT_HALF, tm), BF16)],
        compiler_params=pltpu.CompilerParams(dimension_semantics=("arbitrary", "arbitrary"),
                                             vmem_limit_bytes=VMEM_LIMIT),
        name="experts",
    )(n2t, c_t, p1_t, c_t, p1_t, r2_t, p2_t, u_bf, vt_bf, h2d, mod3, gf.reshape(1, D))


def _expand_keys(sub_keys):
    h, _, k, dh = sub_keys.shape
    eye = jnp.eye(h, dtype=sub_keys.dtype)
    big = jnp.einsum("hpkd,hg->pkhgd", sub_keys, eye)
    return big.reshape(2, k * h, h * dh)


def kernel(x, c, w_ada, b_ada, g_norm1, w_in, conv_dw_w, conv_dw_b, conv_ln_g, conv_ln_b,
           w_conv_out, sgu_ln_g, sgu_ln_b, w_spatial, b_spatial, w_sgu_out, w_out, g_norm2,
           w_query, sub_keys, expert_u, expert_v, g_final):
    bsz, seq, _ = x.shape
    depth = w_ada.shape[0]
    h = x
    for l in range(depth):
        mod3 = _adaln(c, w_ada[l], b_ada[l]).reshape(bsz, 6, D)
        bsp_full = jnp.repeat(b_spatial[l].T, SGU_P, axis=1)
        h = _sub1(h, mod3, g_norm1[l], w_in[l].astype(BF16), conv_dw_w[l], conv_dw_b[l],
                  conv_ln_g[l], conv_ln_b[l], w_conv_out[l].astype(BF16), sgu_ln_g[l],
                  sgu_ln_b[l], w_spatial[l], bsp_full, w_sgu_out[l].astype(BF16),
                  w_out[l].astype(BF16))
        wqt = (w_query[l].reshape(D, HEADS, 2, NKEYS).transpose(2, 1, 3, 0)
               .reshape(2 * HEADS * NKEYS, D).astype(BF16))
        kbig = _expand_keys(sub_keys[l]).astype(BF16)
        h2d = h.reshape(bsz * seq, D)
        n2t, c_t, p1_t, r2_t, p2_t = _route(h2d, mod3, g_norm2[l], wqt, kbig, seq)
        last = l == depth - 1
        gf = g_final if last else jnp.ones_like(g_final)
        out = _experts(n2t, c_t, p1_t, r2_t, p2_t, expert_u[l].astype(BF16),
                       expert_v[l].T.astype(BF16), h2d, mod3, gf, seq)
        h = out.reshape(bsz, seq, D)
    return h
```

```python
import functools

import jax
import jax.numpy as jnp
from jax import lax
from jax.experimental import pallas as pl
from jax.experimental.pallas import tpu as pltpu

F32 = jnp.float32
BF16 = jnp.bfloat16

D = 1024
EPS = 1e-6
CONV_W = 31
HALO = 32
SGU_G = 8
SGU_P = 128
CHUNK = 64
HEADS = 8
NKEYS = 128
TOPK = 16
NEXP = NKEYS * NKEYS
LANES = 128
SUBLANES = 8
VMEM_LIMIT = 56 * 1024 * 1024

NEG_INF = float("-inf")


def _sigmoid(x):
    return 1.0 / (1.0 + jnp.exp(-x))


def _gelu_tanh(x):
    k = 0.7978845608028654
    t = jnp.tanh(x * (k + (k * 0.044715) * (x * x)))
    return x * (0.5 + 0.5 * t)


def _const_spec(shape):
    nd = len(shape)
    return pl.BlockSpec(shape, lambda *_: (0,) * nd, pipeline_mode=pl.Buffered(1))


def _adaln_kernel(c_ref, w_ref, b_ref, o_ref):
    c = c_ref[...]
    ca = c * _sigmoid(c)
    o_ref[...] = jnp.dot(ca, w_ref[...], precision=lax.Precision.HIGHEST,
                         preferred_element_type=F32) + b_ref[...]


def _adaln(c, w, b, tn=1536):
    bsz, n = c.shape[0], w.shape[1]
    return pl.pallas_call(
        _adaln_kernel,
        grid=(n // tn,),
        in_specs=[pl.BlockSpec((bsz, D), lambda j: (0, 0)),
                  pl.BlockSpec((D, tn), lambda j: (0, j)),
                  pl.BlockSpec((1, tn), lambda j: (0, j))],
        out_specs=pl.BlockSpec((bsz, tn), lambda j: (0, j)),
        out_shape=jax.ShapeDtypeStruct((bsz, n), F32),
        compiler_params=pltpu.CompilerParams(dimension_semantics=("arbitrary",),
                                             vmem_limit_bytes=VMEM_LIMIT),
        name="adaln",
    )(c, w, b.reshape(1, n))


def _layer_norm_rows(x, g, b):
    mu = jnp.mean(x, axis=-1, keepdims=True)
    xc = x - mu
    var = jnp.mean(xc * xc, axis=-1, keepdims=True)
    return xc * lax.rsqrt(var + EPS) * g + b


def _sub1_kernel(x_ref, mod_ref, g1_ref, win_ref, dww_ref, dwb_ref, clg_ref, clb_ref, wco_ref,
                 slg_ref, slb_ref, wsp_ref, bsp_ref, wso_ref, wout_ref, h_ref,
                 zbuf, zshift, abuf, vbuf, mbuf, *, conv_rows):
    ts = x_ref.shape[1]
    s = pl.program_id(1)

    @pl.when(s == 0)
    def _():
        zbuf[0:HALO, :] = jnp.zeros((HALO, D), F32)
        zbuf[HALO + ts:HALO + ts + SUBLANES, :] = jnp.zeros((SUBLANES, D), F32)

    x = x_ref[0]
    shift1 = mod_ref[0, 0:1, :]
    scale1 = mod_ref[0, 1:2, :]
    gate1 = mod_ref[0, 2:3, :]
    rn = x * lax.rsqrt(jnp.mean(x * x, axis=-1, keepdims=True) + EPS) * g1_ref[...]
    n = (rn * (1.0 + scale1) + shift1).astype(BF16)

    def proj(c):
        return jnp.dot(n, win_ref[:, c * D:(c + 1) * D], preferred_element_type=F32)

    zbuf[HALO:HALO + ts, :] = proj(0) * _sigmoid(proj(1))

    for sh in range(1, SUBLANES):
        zshift[sh - 1] = zbuf[sh:sh + HALO + ts, :]

    for ci in range(ts // conv_rows):
        r0 = ci * conv_rows
        acc = jnp.broadcast_to(dwb_ref[...], (conv_rows, D))
        for k in range(CONV_W):
            off = HALO - (CONV_W - 1) + k
            base, sh = (off // SUBLANES) * SUBLANES, off % SUBLANES
            rows = slice(r0 + base, r0 + base + conv_rows)
            tap = zbuf[rows, :] if sh == 0 else zshift[sh - 1, rows, :]
            acc = acc + tap * dww_ref[k:k + 1, :]
        a = _layer_norm_rows(acc, clg_ref[...], clb_ref[...])
        abuf[r0:r0 + conv_rows, :] = (a * _sigmoid(a)).astype(BF16)
    zbuf[0:HALO, :] = zbuf[ts:ts + HALO, :]
    y_a = jnp.dot(abuf[...], wco_ref[...], preferred_element_type=F32)
    merged = _sigmoid(proj(4)) * y_a

    vbuf[...] = _layer_norm_rows(proj(3), slg_ref[...], slb_ref[...]).astype(BF16)
    pos_chunk_i = lax.broadcasted_iota(jnp.int32, (SGU_P, SGU_P), 0) // CHUNK
    pos_chunk_j = lax.broadcasted_iota(jnp.int32, (SGU_P, SGU_P), 1) // CHUNK
    keep = pos_chunk_j <= pos_chunk_i
    for g in range(SGU_G):
        wm = jnp.where(keep, wsp_ref[g], 0.0).astype(BF16)
        cols = slice(g * SGU_P, (g + 1) * SGU_P)
        for blk in range(ts // SGU_P):
            rows = slice(blk * SGU_P, (blk + 1) * SGU_P)
            mixed = jnp.dot(wm, vbuf[rows, cols], preferred_element_type=F32)
            mbuf[rows, cols] = mixed + bsp_ref[:, cols]
    gated = (proj(2) * mbuf[...]).astype(BF16)
    y_b = jnp.dot(gated, wso_ref[...], preferred_element_type=F32)
    merged = (merged + _sigmoid(proj(5)) * y_b).astype(BF16)

    o1 = jnp.dot(merged, wout_ref[...], preferred_element_type=F32)
    h_ref[0] = x + gate1 * o1


def _sub1(x, mod3, g1, win, dww, dwb, clg, clb, wco, slg, slb, wsp, bsp_full, wso, wout,
          ts=256, conv_rows=32):
    bsz, seq, _ = x.shape
    row = lambda a: a.reshape(1, D)
    kern = functools.partial(_sub1_kernel, conv_rows=conv_rows)
    return pl.pallas_call(
        kern,
        grid=(bsz, seq // ts),
        in_specs=[pl.BlockSpec((1, ts, D), lambda b, s: (b, s, 0)),
                  pl.BlockSpec((1, 6, D), lambda b, s: (b, 0, 0)),
                  _const_spec((1, D)),
                  _const_spec((D, 6 * D)),
                  _const_spec((CONV_W, D)),
                  _const_spec((1, D)),
                  _const_spec((1, D)),
                  _const_spec((1, D)),
                  _const_spec((D, D)),
                  _const_spec((1, D)),
                  _const_spec((1, D)),
                  _const_spec((SGU_G, SGU_P, SGU_P)),
                  _const_spec((SGU_P, D)),
                  _const_spec((D, D)),
                  _const_spec((D, D))],
        out_specs=pl.BlockSpec((1, ts, D), lambda b, s: (b, s, 0)),
        out_shape=jax.ShapeDtypeStruct((bsz, seq, D), F32),
        scratch_shapes=[pltpu.VMEM((HALO + ts + SUBLANES, D), F32),
                        pltpu.VMEM((SUBLANES - 1, HALO + ts, D), F32),
                        pltpu.VMEM((ts, D), BF16),
                        pltpu.VMEM((ts, D), BF16),
                        pltpu.VMEM((ts, D), F32)],
        compiler_params=pltpu.CompilerParams(dimension_semantics=("arbitrary", "arbitrary"),
                                             vmem_limit_bytes=VMEM_LIMIT),
        name="sub1",
    )(x, mod3, row(g1), win, dww, row(dwb), row(clg), row(clb), wco, row(slg), row(slb),
      wsp, bsp_full, wso, wout)


def _tree(op, xs):
    xs = list(xs)
    while len(xs) > 1:
        nxt = [op(xs[i], xs[i + 1]) for i in range(0, len(xs) - 1, 2)]
        if len(xs) % 2:
            nxt.append(xs[-1])
        xs = nxt
    return xs[0]


def _extract_topk(fill_work, work, rank, vals, n):
    def key_rows(k):
        return pl.ds(k * SUBLANES, SUBLANES)

    def run(lowest_index_first):
        fill_work()
        for k in range(n):
            rank[key_rows(k), :] = jnp.full((SUBLANES, LANES), float(TOPK), F32)

        def round_body(r, carry):
            s = [work[key_rows(k), :] for k in range(n)]
            m = _tree(jnp.maximum, s)
            vals[pl.ds(pl.multiple_of(r * SUBLANES, SUBLANES), SUBLANES), :] = m
            rf = lax.convert_element_type(r, F32)
            if lowest_index_first:
                first = _tree(jnp.minimum,
                              [jnp.where(s[k] == m, float(k), float(n)) for k in range(n)])
            for k in range(n):
                hit = (first == float(k)) if lowest_index_first else (s[k] == m)
                work[key_rows(k), :] = jnp.where(hit, NEG_INF, s[k])
                rank[key_rows(k), :] = jnp.where(hit, rf, rank[key_rows(k), :])
            return carry

        lax.fori_loop(0, TOPK, round_body, 0)

    run(lowest_index_first=False)
    ranked = _tree(jnp.add, [jnp.where(rank[key_rows(k), :] < float(TOPK), 1.0, 0.0)
                             for k in range(n)])
    tie_seen = jnp.max(jnp.abs(ranked - float(TOPK))) > 0.0

    @pl.when(tie_seen)
    def _():
        run(lowest_index_first=True)


def _route_kernel(h_ref, mod_ref, g2_ref, wqt_ref, kbig_ref,
                  n2t_ref, c_ref, p1_ref, r2_ref, p2_ref,
                  s1_scr, s2_scr, w1, w2, rk1, rk2, v1, v2, o_c, o_p1, o_p2):
    tr = h_ref.shape[0]
    h = h_ref[...]
    shift2 = mod_ref[0, 3:4, :]
    scale2 = mod_ref[0, 4:5, :]
    rn = h * lax.rsqrt(jnp.mean(h * h, axis=-1, keepdims=True) + EPS) * g2_ref[...]
    n2 = rn * (1.0 + scale2) + shift2
    n2t = n2.T.astype(BF16)
    n2t_ref[...] = n2t
    qt = jnp.dot(wqt_ref[...], n2t, preferred_element_type=F32).astype(BF16)
    half = HEADS * NKEYS
    s1_scr[...] = jnp.dot(kbig_ref[0], qt[0:half], preferred_element_type=F32)
    s2_scr[...] = jnp.dot(kbig_ref[1], qt[half:2 * half], preferred_element_type=F32)

    cands = [(a, b) for a in range(TOPK) for b in range(TOPK) if (a + 1) * (b + 1) <= TOPK]

    def block_body(tb, carry):
        lane = pl.ds(pl.multiple_of(tb * LANES, LANES), LANES)
        def fill_w1():
            w1[...] = s1_scr[:, lane]

        def fill_w2():
            w2[...] = s2_scr[:, lane]

        _extract_topk(fill_w1, w1, rk1, v1, NKEYS)
        _extract_topk(fill_w2, w2, rk2, v2, NKEYS)

        def rows(k):
            return pl.ds(k * SUBLANES, SUBLANES)

        av = [v1[rows(a), :] for a in range(TOPK)]
        bv = [v2[rows(b), :] for b in range(TOPK)]
        cv = [av[a] + bv[b] for (a, b) in cands]
        nc = len(cv)
        z = None
        m0 = None
        for r in range(TOPK):
            m = _tree(jnp.maximum, cv)
            first = _tree(jnp.minimum,
                          [jnp.where(cv[j] == m, float(j), float(nc)) for j in range(nc)])
            cv = [jnp.where(first == float(j), NEG_INF, cv[j]) for j in range(nc)]
            if r == 0:
                m0 = m
                z = jnp.ones_like(m)
            else:
                z = z + jnp.exp(m - m0)
        inv_z = 1.0 / z
        cnt = []
        for a in range(TOPK):
            sel = [jnp.where(cv[j] == NEG_INF, 1.0, 0.0) for j, (ca, _) in enumerate(cands) if ca == a]
            cnt.append(_tree(jnp.add, sel))
        for k in range(NKEYS):
            rk = rk1[rows(k), :]
            ck = jnp.zeros((SUBLANES, LANES), F32)
            for a in range(TOPK):
                ck = jnp.where(rk == float(a), cnt[a], ck)
            o_c[rows(k), :] = ck
            o_p1[rows(k), :] = jnp.exp(s1_scr[rows(k), lane] - av[0]) * inv_z
            o_p2[rows(k), :] = jnp.exp(s2_scr[rows(k), lane] - bv[0])
        for hd in range(HEADS):
            sel_rows = pl.ds(hd, NKEYS, stride=HEADS)
            c_hd = o_c[sel_rows, :]
            p1_hd = o_p1[sel_rows, :]
            c_ref[hd, :, lane] = pltpu.pack_elementwise([c_hd, c_hd], packed_dtype=BF16)
            p1_ref[hd, :, lane] = pltpu.pack_elementwise([p1_hd, p1_hd], packed_dtype=BF16)
            even = pl.ds(hd, NKEYS // 2, stride=2 * HEADS)
            odd = pl.ds(hd + HEADS, NKEYS // 2, stride=2 * HEADS)
            r2_ref[hd, :, lane] = pltpu.pack_elementwise([rk2[even, :], rk2[odd, :]],
                                                         packed_dtype=BF16)
            p2_ref[hd, :, lane] = pltpu.pack_elementwise([o_p2[even, :], o_p2[odd, :]],
                                                         packed_dtype=BF16)
        return carry

    lax.fori_loop(0, tr // LANES, block_body, 0)


def _route(h2d, mod3, g2, wqt, kbig, seq, tr=512):
    t = h2d.shape[0]
    tbl = jax.ShapeDtypeStruct((HEADS, NKEYS, t), jnp.uint32)
    tbl_pair = jax.ShapeDtypeStruct((HEADS, NKEYS // 2, t), jnp.uint32)
    tbl_spec = pl.BlockSpec((HEADS, NKEYS, tr), lambda i: (0, 0, i))
    pair_spec = pl.BlockSpec((HEADS, NKEYS // 2, tr), lambda i: (0, 0, i))
    scr = lambda n: pltpu.VMEM((n * SUBLANES, LANES), F32)
    return pl.pallas_call(
        _route_kernel,
        grid=(t // tr,),
        in_specs=[pl.BlockSpec((tr, D), lambda i: (i, 0)),
                  pl.BlockSpec((1, 6, D), lambda i: ((i * tr) // seq, 0, 0)),
                  _const_spec((1, D)),
                  _const_spec((2 * HEADS * NKEYS, D)),
                  _const_spec((2, HEADS * NKEYS, HEADS * NKEYS))],
        out_specs=[pl.BlockSpec((D, tr), lambda i: (0, i)),
                   tbl_spec, tbl_spec, pair_spec, pair_spec],
        out_shape=[jax.ShapeDtypeStruct((D, t), BF16), tbl, tbl, tbl_pair, tbl_pair],
        scratch_shapes=[pltpu.VMEM((HEADS * NKEYS, tr), F32),
                        pltpu.VMEM((HEADS * NKEYS, tr), F32),
                        scr(NKEYS), scr(NKEYS), scr(NKEYS), scr(NKEYS), scr(TOPK), scr(TOPK),
                        scr(NKEYS), scr(NKEYS), scr(NKEYS)],
        compiler_params=pltpu.CompilerParams(dimension_semantics=("arbitrary",),
                                             vmem_limit_bytes=VMEM_LIMIT),
        name="route",
    )(h2d, mod3, g2.reshape(1, D), wqt, kbig)


EXPERT_HALF = SUBLANES * NKEYS
PACK = 2 * SUBLANES


def _routing_weights_times_gelu(tb, ils, a_ref, out_ref, c_ref, p1_ref, r2_ref, p2_ref):
    if True:
        lane = slice(tb * LANES, (tb + 1) * LANES)
        c8 = [c_ref[hd, :, lane] for hd in range(HEADS)]
        p8 = [p1_ref[hd, :, lane] for hd in range(HEADS)]
        for il in ils:
            cb = [pltpu.bitcast(jnp.broadcast_to(c8[hd][il:il + 1, :], (SUBLANES, LANES)), BF16)
                  for hd in range(HEADS)]
            pb = [pltpu.bitcast(jnp.broadcast_to(p8[hd][il:il + 1, :], (SUBLANES, LANES)), BF16)
                  for hd in range(HEADS)]
            for jb in range(NKEYS // PACK):
                rows = slice(il * NKEYS + jb * PACK, il * NKEYS + (jb + 1) * PACK)
                jrows = slice(jb * SUBLANES, (jb + 1) * SUBLANES)
                g = _gelu_tanh(a_ref[rows, lane]).astype(BF16)
                w = jnp.zeros((PACK, LANES), BF16)
                for hd in range(HEADS):
                    r2 = pltpu.bitcast(r2_ref[hd, jrows, lane], BF16)
                    p2 = pltpu.bitcast(p2_ref[hd, jrows, lane], BF16)
                    w = w + jnp.where(r2 < cb[hd], p2 * pb[hd], jnp.zeros_like(p2))
                out_ref[rows, lane] = g * w


def _experts_kernel(n2t_ref, c_odd_ref, p1_odd_ref, c_even_ref, p1_even_ref, r2_ref, p2_ref,
                    u_ref, vt_ref, h_ref, mod_ref, gf_ref, o_ref, acc, a0, a1, h0, h1):
    q = pl.program_id(1)
    nq = pl.num_programs(1) - 1

    @pl.when(q == 0)
    def _():
        acc[...] = jnp.zeros_like(acc)
        a1[...] = jnp.zeros_like(a1)
        h0[...] = jnp.zeros_like(h0)
        h1[...] = jnp.zeros_like(h1)

    tm = n2t_ref.shape[1]
    mxu_n = 2 * LANES
    row_pieces = 2
    piece_rows = EXPERT_HALF // row_pieces
    n_pieces = row_pieces * (tm // mxu_n)
    ew_per_tb = n_pieces // (tm // LANES)
    il_per_piece = SUBLANES // ew_per_tb

    def stage(e0, a_dst, a_src, h_dst, h_src, c_ref, p1_ref):
        pieces = [(r, n) for n in range(tm // mxu_n) for r in range(row_pieces)]
        for p, (r, n) in enumerate(pieces):
            rows = slice(r * piece_rows, (r + 1) * piece_rows)
            cols = slice(n * mxu_n, (n + 1) * mxu_n)
            a_dst[rows, cols] = jnp.dot(u_ref[e0 + r * piece_rows:e0 + (r + 1) * piece_rows, :],
                                        n2t_ref[:, cols], preferred_element_type=F32)
            ils = range((p % ew_per_tb) * il_per_piece, (p % ew_per_tb + 1) * il_per_piece)
            _routing_weights_times_gelu(p // ew_per_tb, ils, a_src, h_dst, c_ref, p1_ref,
                                        r2_ref, p2_ref)
            acc[rows, cols] += jnp.dot(vt_ref[rows, e0:e0 + EXPERT_HALF], h_src[:, cols],
                                       preferred_element_type=F32)

    stage(0, a0, a1, h1, h0, c_odd_ref, p1_odd_ref)
    stage(EXPERT_HALF, a1, a0, h0, h1, c_even_ref, p1_even_ref)

    @pl.when(q == nq)
    def _():
        gate2 = mod_ref[0, 5:6, :]
        h2 = h_ref[...] + gate2 * acc[...].T
        o_ref[...] = (h2 * lax.rsqrt(jnp.mean(h2 * h2, axis=-1, keepdims=True) + EPS)
                      * gf_ref[...])


def _experts(n2t, c_t, p1_t, r2_t, p2_t, u_bf, vt_bf, h2d, mod3, gf, seq, tm=512):
    t = h2d.shape[0]
    te = 2 * EXPERT_HALF
    nq = NEXP // te
    last_group = NKEYS // SUBLANES - 1
    odd = pl.BlockSpec((HEADS, SUBLANES, tm),
                       lambda i, q: (0, jnp.clip(2 * q - 1, 0, last_group), i))
    even = pl.BlockSpec((HEADS, SUBLANES, tm), lambda i, q: (0, jnp.minimum(2 * q, last_group), i))
    pair = pl.BlockSpec((HEADS, NKEYS // 2, tm), lambda i, q: (0, 0, i))
    return pl.pallas_call(
        _experts_kernel,
        grid=(t // tm, nq + 1),
        in_specs=[pl.BlockSpec((D, tm), lambda i, q: (0, i)),
                  odd, odd, even, even, pair, pair,
                  pl.BlockSpec((te, D), lambda i, q: (jnp.minimum(q, nq - 1), 0)),
                  pl.BlockSpec((D, te), lambda i, q: (0, jnp.maximum(q - 1, 0))),
                  pl.BlockSpec((tm, D), lambda i, q: (i, 0)),
                  pl.BlockSpec((1, 6, D), lambda i, q: ((i * tm) // seq, 0, 0)),
                  _const_spec((1, D))],
        out_specs=pl.BlockSpec((tm, D), lambda i, q: (i, 0)),
        out_shape=jax.ShapeDtypeStruct((t, D), F32),
        scratch_shapes=[pltpu.VMEM((D, tm), F32),
                        pltpu.VMEM((EXPERT_HALF, tm), F32),
                        pltpu.VMEM((EXPERT_HALF, tm), F32),
                        pltpu.VMEM((EXPERT_HALF, tm), BF16),
                        pltpu.VMEM((EXPERT_HALF, tm), BF16)],
        compiler_params=pltpu.CompilerParams(dimension_semantics=("arbitrary", "arbitrary"),
                                             vmem_limit_bytes=VMEM_LIMIT),
        name="experts",
    )(n2t, c_t, p1_t, c_t, p1_t, r2_t, p2_t, u_bf, vt_bf, h2d, mod3, gf.reshape(1, D))


def _expand_keys(sub_keys):
    h, _, k, dh = sub_keys.shape
    eye = jnp.eye(h, dtype=sub_keys.dtype)
    big = jnp.einsum("hpkd,hg->pkhgd", sub_keys, eye)
    return big.reshape(2, k * h, h * dh)


def kernel(x, c, w_ada, b_ada, g_norm1, w_in, conv_dw_w, conv_dw_b, conv_ln_g, conv_ln_b,
           w_conv_out, sgu_ln_g, sgu_ln_b, w_spatial, b_spatial, w_sgu_out, w_out, g_norm2,
           w_query, sub_keys, expert_u, expert_v, g_final):
    bsz, seq, _ = x.shape
    depth = w_ada.shape[0]
    h = x
    for l in range(depth):
        mod3 = _adaln(c, w_ada[l], b_ada[l]).reshape(bsz, 6, D)
        bsp_full = jnp.repeat(b_spatial[l].T, SGU_P, axis=1)
        h = _sub1(h, mod3, g_norm1[l], w_in[l].astype(BF16), conv_dw_w[l], conv_dw_b[l],
                  conv_ln_g[l], conv_ln_b[l], w_conv_out[l].astype(BF16), sgu_ln_g[l],
                  sgu_ln_b[l], w_spatial[l], bsp_full, w_sgu_out[l].astype(BF16),
                  w_out[l].astype(BF16))
        wqt = (w_query[l].reshape(D, HEADS, 2, NKEYS).transpose(2, 1, 3, 0)
               .reshape(2 * HEADS * NKEYS, D).astype(BF16))
        kbig = _expand_keys(sub_keys[l]).astype(BF16)
        h2d = h.reshape(bsz * seq, D)
        n2t, c_t, p1_t, r2_t, p2_t = _route(h2d, mod3, g_norm2[l], wqt, kbig, seq)
        last = l == depth - 1
        gf = g_final if last else jnp.ones_like(g_final)
        out = _experts(n2t, c_t, p1_t, r2_t, p2_t, expert_u[l].astype(BF16),
                       expert_v[l].T.astype(BF16), h2d, mod3, gf, seq)
        h = out.reshape(bsz, seq, D)
    return h
```

```python
import functools

import jax
import jax.numpy as jnp
from jax import lax
from jax.experimental import pallas as pl
from jax.experimental.pallas import tpu as pltpu

F32 = jnp.float32
BF16 = jnp.bfloat16

D = 1024
EPS = 1e-6
CONV_W = 31
HALO = 32
SGU_G = 8
SGU_P = 128
CHUNK = 64
HEADS = 8
NKEYS = 128
TOPK = 16
NEXP = NKEYS * NKEYS
LANES = 128
SUBLANES = 8
VMEM_LIMIT = 56 * 1024 * 1024

NEG_INF = float("-inf")


def _sigmoid(x):
    return 1.0 / (1.0 + jnp.exp(-x))


def _gelu_tanh(x):
    k = 0.7978845608028654
    t = jnp.tanh(x * (k + (k * 0.044715) * (x * x)))
    return x * (0.5 + 0.5 * t)


def _const_spec(shape):
    nd = len(shape)
    return pl.BlockSpec(shape, lambda *_: (0,) * nd, pipeline_mode=pl.Buffered(1))


def _adaln_kernel(c_ref, w_ref, b_ref, o_ref):
    c = c_ref[...]
    ca = c * _sigmoid(c)
    o_ref[...] = jnp.dot(ca, w_ref[...], precision=lax.Precision.HIGHEST,
                         preferred_element_type=F32) + b_ref[...]


def _adaln(c, w, b, tn=1536):
    bsz, n = c.shape[0], w.shape[1]
    return pl.pallas_call(
        _adaln_kernel,
        grid=(n // tn,),
        in_specs=[pl.BlockSpec((bsz, D), lambda j: (0, 0)),
                  pl.BlockSpec((D, tn), lambda j: (0, j)),
                  pl.BlockSpec((1, tn), lambda j: (0, j))],
        out_specs=pl.BlockSpec((bsz, tn), lambda j: (0, j)),
        out_shape=jax.ShapeDtypeStruct((bsz, n), F32),
        compiler_params=pltpu.CompilerParams(dimension_semantics=("arbitrary",),
                                             vmem_limit_bytes=VMEM_LIMIT),
        name="adaln",
    )(c, w, b.reshape(1, n))


def _layer_norm_rows(x, g, b):
    mu = jnp.mean(x, axis=-1, keepdims=True)
    xc = x - mu
    var = jnp.mean(xc * xc, axis=-1, keepdims=True)
    return xc * lax.rsqrt(var + EPS) * g + b


def _sub1_kernel(x_ref, mod_ref, g1_ref, win_ref, dww_ref, dwb_ref, clg_ref, clb_ref, wco_ref,
                 slg_ref, slb_ref, wsp_ref, bsp_ref, wso_ref, wout_ref, h_ref,
                 zbuf, zshift, abuf, vbuf, mbuf, *, conv_rows):
    ts = x_ref.shape[1]
    s = pl.program_id(1)

    @pl.when(s == 0)
    def _():
        zbuf[0:HALO, :] = jnp.zeros((HALO, D), F32)
        zbuf[HALO + ts:HALO + ts + SUBLANES, :] = jnp.zeros((SUBLANES, D), F32)

    x = x_ref[0]
    shift1 = mod_ref[0, 0:1, :]
    scale1 = mod_ref[0, 1:2, :]
    gate1 = mod_ref[0, 2:3, :]
    rn = x * lax.rsqrt(jnp.mean(x * x, axis=-1, keepdims=True) + EPS) * g1_ref[...]
    n = (rn * (1.0 + scale1) + shift1).astype(BF16)

    def proj(c):
        return jnp.dot(n, win_ref[:, c * D:(c + 1) * D], preferred_element_type=F32)

    zbuf[HALO:HALO + ts, :] = proj(0) * _sigmoid(proj(1))

    for sh in range(1, SUBLANES):
        zshift[sh - 1] = zbuf[sh:sh + HALO + ts, :]

    for ci in range(ts // conv_rows):
        r0 = ci * conv_rows
        acc = jnp.broadcast_to(dwb_ref[...], (conv_rows, D))
        for k in range(CONV_W):
            off = HALO - (CONV_W - 1) + k
            base, sh = (off // SUBLANES) * SUBLANES, off % SUBLANES
            rows = slice(r0 + base, r0 + base + conv_rows)
            tap = zbuf[rows, :] if sh == 0 else zshift[sh - 1, rows, :]
            acc = acc + tap * dww_ref[k:k + 1, :]
        a = _layer_norm_rows(acc, clg_ref[...], clb_ref[...])
        abuf[r0:r0 + conv_rows, :] = (a * _sigmoid(a)).astype(BF16)
    zbuf[0:HALO, :] = zbuf[ts:ts + HALO, :]
    y_a = jnp.dot(abuf[...], wco_ref[...], preferred_element_type=F32)
    merged = _sigmoid(proj(4)) * y_a

    vbuf[...] = _layer_norm_rows(proj(3), slg_ref[...], slb_ref[...]).astype(BF16)
    pos_chunk_i = lax.broadcasted_iota(jnp.int32, (SGU_P, SGU_P), 0) // CHUNK
    pos_chunk_j = lax.broadcasted_iota(jnp.int32, (SGU_P, SGU_P), 1) // CHUNK
    keep = pos_chunk_j <= pos_chunk_i
    for g in range(SGU_G):
        wm = jnp.where(keep, wsp_ref[g], 0.0).astype(BF16)
        cols = slice(g * SGU_P, (g + 1) * SGU_P)
        for blk in range(ts // SGU_P):
            rows = slice(blk * SGU_P, (blk + 1) * SGU_P)
            mixed = jnp.dot(wm, vbuf[rows, cols], preferred_element_type=F32)
            mbuf[rows, cols] = mixed + bsp_ref[:, cols]
    gated = (proj(2) * mbuf[...]).astype(BF16)
    y_b = jnp.dot(gated, wso_ref[...], preferred_element_type=F32)
    merged = (merged + _sigmoid(proj(5)) * y_b).astype(BF16)

    o1 = jnp.dot(merged, wout_ref[...], preferred_element_type=F32)
    h_ref[0] = x + gate1 * o1


def _sub1(x, mod3, g1, win, dww, dwb, clg, clb, wco, slg, slb, wsp, bsp_full, wso, wout,
          ts=256, conv_rows=32):
    bsz, seq, _ = x.shape
    row = lambda a: a.reshape(1, D)
    kern = functools.partial(_sub1_kernel, conv_rows=conv_rows)
    return pl.pallas_call(
        kern,
        grid=(bsz, seq // ts),
        in_specs=[pl.BlockSpec((1, ts, D), lambda b, s: (b, s, 0)),
                  pl.BlockSpec((1, 6, D), lambda b, s: (b, 0, 0)),
                  _const_spec((1, D)),
                  _const_spec((D, 6 * D)),
                  _const_spec((CONV_W, D)),
                  _const_spec((1, D)),
                  _const_spec((1, D)),
                  _const_spec((1, D)),
                  _const_spec((D, D)),
                  _const_spec((1, D)),
                  _const_spec((1, D)),
                  _const_spec((SGU_G, SGU_P, SGU_P)),
                  _const_spec((SGU_P, D)),
                  _const_spec((D, D)),
                  _const_spec((D, D))],
        out_specs=pl.BlockSpec((1, ts, D), lambda b, s: (b, s, 0)),
        out_shape=jax.ShapeDtypeStruct((bsz, seq, D), F32),
        scratch_shapes=[pltpu.VMEM((HALO + ts + SUBLANES, D), F32),
                        pltpu.VMEM((SUBLANES - 1, HALO + ts, D), F32),
                        pltpu.VMEM((ts, D), BF16),
                        pltpu.VMEM((ts, D), BF16),
                        pltpu.VMEM((ts, D), F32)],
        compiler_params=pltpu.CompilerParams(dimension_semantics=("arbitrary", "arbitrary"),
                                             vmem_limit_bytes=VMEM_LIMIT),
        name="sub1",
    )(x, mod3, row(g1), win, dww, row(dwb), row(clg), row(clb), wco, row(slg), row(slb),
      wsp, bsp_full, wso, wout)


def _tree(op, xs):
    xs = list(xs)
    while len(xs) > 1:
        nxt = [op(xs[i], xs[i + 1]) for i in range(0, len(xs) - 1, 2)]
        if len(xs) % 2:
            nxt.append(xs[-1])
        xs = nxt
    return xs[0]


def _extract_topk(fill_work, work, rank, vals, n):
    def key_rows(k):
        return pl.ds(k * SUBLANES, SUBLANES)

    def run(lowest_index_first):
        fill_work()
        for k in range(n):
            rank[key_rows(k), :] = jnp.full((SUBLANES, LANES), float(TOPK), F32)

        def round_body(r, carry):
            s = [work[key_rows(k), :] for k in range(n)]
            m = _tree(jnp.maximum, s)
            vals[pl.ds(pl.multiple_of(r * SUBLANES, SUBLANES), SUBLANES), :] = m
            rf = lax.convert_element_type(r, F32)
            if lowest_index_first:
                first = _tree(jnp.minimum,
                              [jnp.where(s[k] == m, float(k), float(n)) for k in range(n)])
            for k in range(n):
                hit = (first == float(k)) if lowest_index_first else (s[k] == m)
                work[key_rows(k), :] = jnp.where(hit, NEG_INF, s[k])
                rank[key_rows(k), :] = jnp.where(hit, rf, rank[key_rows(k), :])
            return carry

        lax.fori_loop(0, TOPK, round_body, 0)

    run(lowest_index_first=False)
    ranked = _tree(jnp.add, [jnp.where(rank[key_rows(k), :] < float(TOPK), 1.0, 0.0)
                             for k in range(n)])
    tie_seen = jnp.max(jnp.abs(ranked - float(TOPK))) > 0.0

    @pl.when(tie_seen)
    def _():
        run(lowest_index_first=True)


def _route_kernel(h_ref, mod_ref, g2_ref, wqt_ref, kbig_ref,
                  n2t_ref, c_ref, p1_ref, r2_ref, p2_ref,
                  s1_scr, s2_scr, w1, w2, rk1, rk2, v1, v2, o_c, o_p1, o_p2):
    tr = h_ref.shape[0]
    h = h_ref[...]
    shift2 = mod_ref[0, 3:4, :]
    scale2 = mod_ref[0, 4:5, :]
    rn = h * lax.rsqrt(jnp.mean(h * h, axis=-1, keepdims=True) + EPS) * g2_ref[...]
    n2 = rn * (1.0 + scale2) + shift2
    n2t = n2.T.astype(BF16)
    n2t_ref[...] = n2t
    qt = jnp.dot(wqt_ref[...], n2t, preferred_element_type=F32).astype(BF16)
    half = HEADS * NKEYS
    s1_scr[...] = jnp.dot(kbig_ref[0], qt[0:half], preferred_element_type=F32)
    s2_scr[...] = jnp.dot(kbig_ref[1], qt[half:2 * half], preferred_element_type=F32)

    cands = [(a, b) for a in range(TOPK) for b in range(TOPK) if (a + 1) * (b + 1) <= TOPK]

    def block_body(tb, carry):
        lane = pl.ds(pl.multiple_of(tb * LANES, LANES), LANES)
        def fill_w1():
            w1[...] = s1_scr[:, lane]

        def fill_w2():
            w2[...] = s2_scr[:, lane]

        _extract_topk(fill_w1, w1, rk1, v1, NKEYS)
        _extract_topk(fill_w2, w2, rk2, v2, NKEYS)

        def rows(k):
            return pl.ds(k * SUBLANES, SUBLANES)

        av = [v1[rows(a), :] for a in range(TOPK)]
        bv = [v2[rows(b), :] for b in range(TOPK)]
        cv = [av[a] + bv[b] for (a, b) in cands]
        nc = len(cv)
        z = None
        m0 = None
        for r in range(TOPK):
            m = _tree(jnp.maximum, cv)
            first = _tree(jnp.minimum,
                          [jnp.where(cv[j] == m, float(j), float(nc)) for j in range(nc)])
            cv = [jnp.where(first == float(j), NEG_INF, cv[j]) for j in range(nc)]
            if r == 0:
                m0 = m
                z = jnp.ones_like(m)
            else:
                z = z + jnp.exp(m - m0)
        inv_z = 1.0 / z
        cnt = []
        for a in range(TOPK):
            sel = [jnp.where(cv[j] == NEG_INF, 1.0, 0.0) for j, (ca, _) in enumerate(cands) if ca == a]
            cnt.append(_tree(jnp.add, sel))
        for k in range(NKEYS):
            rk = rk1[rows(k), :]
            ck = jnp.zeros((SUBLANES, LANES), F32)
            for a in range(TOPK):
                ck = jnp.where(rk == float(a), cnt[a], ck)
            o_c[rows(k), :] = ck
            o_p1[rows(k), :] = jnp.exp(s1_scr[rows(k), lane] - av[0]) * inv_z
            o_p2[rows(k), :] = jnp.exp(s2_scr[rows(k), lane] - bv[0])
        for hd in range(HEADS):
            sel_rows = pl.ds(hd, NKEYS, stride=HEADS)
            c_hd = o_c[sel_rows, :]
            p1_hd = o_p1[sel_rows, :]
            c_ref[hd, :, lane] = pltpu.pack_elementwise([c_hd, c_hd], packed_dtype=BF16)
            p1_ref[hd, :, lane] = pltpu.pack_elementwise([p1_hd, p1_hd], packed_dtype=BF16)
            even = pl.ds(hd, NKEYS // 2, stride=2 * HEADS)
            odd = pl.ds(hd + HEADS, NKEYS // 2, stride=2 * HEADS)
            r2_ref[hd, :, lane] = pltpu.pack_elementwise([rk2[even, :], rk2[odd, :]],
                                                         packed_dtype=BF16)
            p2_ref[hd, :, lane] = pltpu.pack_elementwise([o_p2[even, :], o_p2[odd, :]],
                                                         packed_dtype=BF16)
        return carry

    lax.fori_loop(0, tr // LANES, block_body, 0)


def _route(h2d, mod3, g2, wqt, kbig, seq, tr=512):
    t = h2d.shape[0]
    tbl = jax.ShapeDtypeStruct((HEADS, NKEYS, t), jnp.uint32)
    tbl_pair = jax.ShapeDtypeStruct((HEADS, NKEYS // 2, t), jnp.uint32)
    tbl_spec = pl.BlockSpec((HEADS, NKEYS, tr), lambda i: (0, 0, i))
    pair_spec = pl.BlockSpec((HEADS, NKEYS // 2, tr), lambda i: (0, 0, i))
    scr = lambda n: pltpu.VMEM((n * SUBLANES, LANES), F32)
    return pl.pallas_call(
        _route_kernel,
        grid=(t // tr,),
        in_specs=[pl.BlockSpec((tr, D), lambda i: (i, 0)),
                  pl.BlockSpec((1, 6, D), lambda i: ((i * tr) // seq, 0, 0)),
                  _const_spec((1, D)),
                  _const_spec((2 * HEADS * NKEYS, D)),
                  _const_spec((2, HEADS * NKEYS, HEADS * NKEYS))],
        out_specs=[pl.BlockSpec((D, tr), lambda i: (0, i)),
                   tbl_spec, tbl_spec, pair_spec, pair_spec],
        out_shape=[jax.ShapeDtypeStruct((D, t), BF16), tbl, tbl, tbl_pair, tbl_pair],
        scratch_shapes=[pltpu.VMEM((HEADS * NKEYS, tr), F32),
                        pltpu.VMEM((HEADS * NKEYS, tr), F32),
                        scr(NKEYS), scr(NKEYS), scr(NKEYS), scr(NKEYS), scr(TOPK), scr(TOPK),
                        scr(NKEYS), scr(NKEYS), scr(NKEYS)],
        compiler_params=pltpu.CompilerParams(dimension_semantics=("arbitrary",),
                                             vmem_limit_bytes=VMEM_LIMIT),
        name="route",
    )(h2d, mod3, g2.reshape(1, D), wqt, kbig)


EXPERT_HALF = SUBLANES * NKEYS
PACK = 2 * SUBLANES


def _routing_weights_times_gelu(tb, ils, a_ref, out_ref, c_ref, p1_ref, r2_ref, p2_ref):
    if True:
        lane = slice(tb * LANES, (tb + 1) * LANES)
        c8 = [c_ref[hd, :, lane] for hd in range(HEADS)]
        p8 = [p1_ref[hd, :, lane] for hd in range(HEADS)]
        for il in ils:
            cb = [pltpu.bitcast(jnp.broadcast_to(c8[hd][il:il + 1, :], (SUBLANES, LANES)), BF16)
                  for hd in range(HEADS)]
            pb = [pltpu.bitcast(jnp.broadcast_to(p8[hd][il:il + 1, :], (SUBLANES, LANES)), BF16)
                  for hd in range(HEADS)]
            for jb in range(NKEYS // PACK):
                rows = slice(il * NKEYS + jb * PACK, il * NKEYS + (jb + 1) * PACK)
                jrows = slice(jb * SUBLANES, (jb + 1) * SUBLANES)
                g = _gelu_tanh(a_ref[rows, lane]).astype(BF16)
                w = jnp.zeros((PACK, LANES), BF16)
                for hd in range(HEADS):
                    r2 = pltpu.bitcast(r2_ref[hd, jrows, lane], BF16)
                    p2 = pltpu.bitcast(p2_ref[hd, jrows, lane], BF16)
                    w = w + jnp.where(r2 < cb[hd], p2 * pb[hd], jnp.zeros_like(p2))
                out_ref[rows, lane] = g * w


def _experts_kernel(n2t_ref, c_odd_ref, p1_odd_ref, r2_odd_ref, p2_odd_ref,
                    c_even_ref, p1_even_ref, r2_even_ref, p2_even_ref,
                    u_ref, vt_ref, h_ref, mod_ref, gf_ref, o_ref, acc, a0, a1, h0, h1,
                    *, steps_per_tile):
    g = pl.program_id(0)
    first_of_tile = g % steps_per_tile == 1

    @pl.when(g == 0)
    def _():
        a1[...] = jnp.zeros_like(a1)
        h0[...] = jnp.zeros_like(h0)
        h1[...] = jnp.zeros_like(h1)

    @pl.when(jnp.logical_or(g == 0, first_of_tile))
    def _():
        acc[...] = jnp.zeros_like(acc)

    tm = n2t_ref.shape[1]
    mxu_n = 2 * LANES
    row_pieces = 2
    piece_rows = EXPERT_HALF // row_pieces
    n_pieces = row_pieces * (tm // mxu_n)
    ew_per_tb = n_pieces // (tm // LANES)
    il_per_piece = SUBLANES // ew_per_tb

    def stage(e0, a_dst, a_src, h_dst, h_src, tables):
        pieces = [(r, n) for n in range(tm // mxu_n) for r in range(row_pieces)]
        for p, (r, n) in enumerate(pieces):
            rows = slice(r * piece_rows, (r + 1) * piece_rows)
            cols = slice(n * mxu_n, (n + 1) * mxu_n)
            a_dst[rows, cols] = jnp.dot(u_ref[e0 + r * piece_rows:e0 + (r + 1) * piece_rows, :],
                                        n2t_ref[:, cols], preferred_element_type=F32)
            ils = range((p % ew_per_tb) * il_per_piece, (p % ew_per_tb + 1) * il_per_piece)
            _routing_weights_times_gelu(p // ew_per_tb, ils, a_src, h_dst, *tables)
            acc[rows, cols] += jnp.dot(vt_ref[rows, e0:e0 + EXPERT_HALF], h_src[:, cols],
                                       preferred_element_type=F32)

    stage(0, a0, a1, h1, h0, (c_odd_ref, p1_odd_ref, r2_odd_ref, p2_odd_ref))
    stage(EXPERT_HALF, a1, a0, h0, h1, (c_even_ref, p1_even_ref, r2_even_ref, p2_even_ref))

    @pl.when(jnp.logical_and(g % steps_per_tile == 0, g > 0))
    def _():
        gate2 = mod_ref[0, 5:6, :]
        h2 = h_ref[...] + gate2 * acc[...].T
        o_ref[...] = (h2 * lax.rsqrt(jnp.mean(h2 * h2, axis=-1, keepdims=True) + EPS)
                      * gf_ref[...])


def _experts(n2t, c_t, p1_t, r2_t, p2_t, u_bf, vt_bf, h2d, mod3, gf, seq, tm=512):
    t = h2d.shape[0]
    te = 2 * EXPERT_HALF
    nq = NEXP // te
    nt = t // tm
    groups = NKEYS // SUBLANES
    assert groups == 2 * nq
    last_half = nt * groups - 1

    def odd_half(g):
        return jnp.maximum(2 * g - 1, 0)

    def even_half(g):
        return jnp.minimum(2 * g, last_half)

    def out_tile(g):
        return jnp.maximum(g - 1, 0) // nq

    rows_spec = lambda half: pl.BlockSpec(
        (HEADS, SUBLANES, tm), lambda g: (0, half(g) % groups, half(g) // groups))
    pair_spec = lambda half: pl.BlockSpec(
        (HEADS, NKEYS // 2, tm), lambda g: (0, 0, half(g) // groups))
    kern = functools.partial(_experts_kernel, steps_per_tile=nq)
    return pl.pallas_call(
        kern,
        grid=(nt * nq + 1,),
        in_specs=[pl.BlockSpec((D, tm), lambda g: (0, jnp.minimum(g // nq, nt - 1))),
                  rows_spec(odd_half), rows_spec(odd_half), pair_spec(odd_half), pair_spec(odd_half),
                  rows_spec(even_half), rows_spec(even_half), pair_spec(even_half),
                  pair_spec(even_half),
                  pl.BlockSpec((te, D), lambda g: (g % nq, 0)),
                  pl.BlockSpec((D, te), lambda g: (0, jnp.maximum(g - 1, 0) % nq)),
                  pl.BlockSpec((tm, D), lambda g: (out_tile(g), 0)),
                  pl.BlockSpec((1, 6, D), lambda g: ((out_tile(g) * tm) // seq, 0, 0)),
                  _const_spec((1, D))],
        out_specs=pl.BlockSpec((tm, D), lambda g: (out_tile(g), 0)),
        out_shape=jax.ShapeDtypeStruct((t, D), F32),
        scratch_shapes=[pltpu.VMEM((D, tm), F32),
                        pltpu.VMEM((EXPERT_HALF, tm), F32),
                        pltpu.VMEM((EXPERT_HALF, tm), F32),
                        pltpu.VMEM((EXPERT_HALF, tm), BF16),
                        pltpu.VMEM((EXPERT_HALF, tm), BF16)],
        compiler_params=pltpu.CompilerParams(dimension_semantics=("arbitrary",),
                                             vmem_limit_bytes=VMEM_LIMIT),
        name="experts",
    )(n2t, c_t, p1_t, r2_t, p2_t, c_t, p1_t, r2_t, p2_t, u_bf, vt_bf, h2d, mod3,
      gf.reshape(1, D))


def _expand_keys(sub_keys):
    h, _, k, dh = sub_keys.shape
    eye = jnp.eye(h, dtype=sub_keys.dtype)
    big = jnp.einsum("hpkd,hg->pkhgd", sub_keys, eye)
    return big.reshape(2, k * h, h * dh)


def kernel(x, c, w_ada, b_ada, g_norm1, w_in, conv_dw_w, conv_dw_b, conv_ln_g, conv_ln_b,
           w_conv_out, sgu_ln_g, sgu_ln_b, w_spatial, b_spatial, w_sgu_out, w_out, g_norm2,
           w_query, sub_keys, expert_u, expert_v, g_final):
    bsz, seq, _ = x.shape
    depth = w_ada.shape[0]
    h = x
    for l in range(depth):
        mod3 = _adaln(c, w_ada[l], b_ada[l]).reshape(bsz, 6, D)
        bsp_full = jnp.repeat(b_spatial[l].T, SGU_P, axis=1)
        h = _sub1(h, mod3, g_norm1[l], w_in[l].astype(BF16), conv_dw_w[l], conv_dw_b[l],
                  conv_ln_g[l], conv_ln_b[l], w_conv_out[l].astype(BF16), sgu_ln_g[l],
                  sgu_ln_b[l], w_spatial[l], bsp_full, w_sgu_out[l].astype(BF16),
                  w_out[l].astype(BF16))
        wqt = (w_query[l].reshape(D, HEADS, 2, NKEYS).transpose(2, 1, 3, 0)
               .reshape(2 * HEADS * NKEYS, D).astype(BF16))
        kbig = _expand_keys(sub_keys[l]).astype(BF16)
        h2d = h.reshape(bsz * seq, D)
        n2t, c_t, p1_t, r2_t, p2_t = _route(h2d, mod3, g_norm2[l], wqt, kbig, seq)
        last = l == depth - 1
        gf = g_final if last else jnp.ones_like(g_final)
        out = _experts(n2t, c_t, p1_t, r2_t, p2_t, expert_u[l].astype(BF16),
                       expert_v[l].astype(BF16).T, h2d, mod3, gf, seq)
        h = out.reshape(bsz, seq, D)
    return h
```

```python
import functools

import jax
import jax.numpy as jnp
from jax import lax
from jax.experimental import pallas as pl
from jax.experimental.pallas import tpu as pltpu

F32 = jnp.float32
BF16 = jnp.bfloat16

D = 1024
EPS = 1e-6
CONV_W = 31
HALO = 32
SGU_G = 8
SGU_P = 128
CHUNK = 64
HEADS = 8
NKEYS = 128
TOPK = 16
NEXP = NKEYS * NKEYS
LANES = 128
SUBLANES = 8
VMEM_LIMIT = 56 * 1024 * 1024

NEG_INF = float("-inf")


def _sigmoid(x):
    return 1.0 / (1.0 + jnp.exp(-x))


def _gelu_tanh(x):
    k = 0.7978845608028654
    t = jnp.tanh(x * (k + (k * 0.044715) * (x * x)))
    return x * (0.5 + 0.5 * t)


def _const_spec(shape):
    nd = len(shape)
    return pl.BlockSpec(shape, lambda *_: (0,) * nd, pipeline_mode=pl.Buffered(1))


def _adaln_kernel(c_ref, w_ref, b_ref, o_ref):
    c = c_ref[...]
    ca = c * _sigmoid(c)
    o_ref[...] = jnp.dot(ca, w_ref[...], precision=lax.Precision.HIGHEST,
                         preferred_element_type=F32) + b_ref[...]


def _adaln(c, w, b, tn=1536):
    bsz, n = c.shape[0], w.shape[1]
    return pl.pallas_call(
        _adaln_kernel,
        grid=(n // tn,),
        in_specs=[pl.BlockSpec((bsz, D), lambda j: (0, 0)),
                  pl.BlockSpec((D, tn), lambda j: (0, j)),
                  pl.BlockSpec((1, tn), lambda j: (0, j))],
        out_specs=pl.BlockSpec((bsz, tn), lambda j: (0, j)),
        out_shape=jax.ShapeDtypeStruct((bsz, n), F32),
        compiler_params=pltpu.CompilerParams(dimension_semantics=("arbitrary",),
                                             vmem_limit_bytes=VMEM_LIMIT),
        name="adaln",
    )(c, w, b.reshape(1, n))


def _layer_norm_rows(x, g, b):
    mu = jnp.mean(x, axis=-1, keepdims=True)
    xc = x - mu
    var = jnp.mean(xc * xc, axis=-1, keepdims=True)
    return xc * lax.rsqrt(var + EPS) * g + b


def _sub1_kernel(x_ref, mod_ref, g1_ref, win_ref, dww_ref, dwb_ref, clg_ref, clb_ref, wco_ref,
                 slg_ref, slb_ref, wsp_ref, bsp_ref, wso_ref, wout_ref, h_ref,
                 zbuf, zshift, abuf, vbuf, mbuf, *, conv_rows):
    ts = x_ref.shape[1]
    s = pl.program_id(1)

    @pl.when(s == 0)
    def _():
        zbuf[0:HALO, :] = jnp.zeros((HALO, D), F32)
        zbuf[HALO + ts:HALO + ts + SUBLANES, :] = jnp.zeros((SUBLANES, D), F32)

    x = x_ref[0]
    shift1 = mod_ref[0, 0:1, :]
    scale1 = mod_ref[0, 1:2, :]
    gate1 = mod_ref[0, 2:3, :]
    rn = x * lax.rsqrt(jnp.mean(x * x, axis=-1, keepdims=True) + EPS) * g1_ref[...]
    n = (rn * (1.0 + scale1) + shift1).astype(BF16)

    def proj(c):
        return jnp.dot(n, win_ref[:, c * D:(c + 1) * D], preferred_element_type=F32)

    zbuf[HALO:HALO + ts, :] = proj(0) * _sigmoid(proj(1))

    for sh in range(1, SUBLANES):
        zshift[sh - 1] = zbuf[sh:sh + HALO + ts, :]

    for ci in range(ts // conv_rows):
        r0 = ci * conv_rows
        acc = jnp.broadcast_to(dwb_ref[...], (conv_rows, D))
        for k in range(CONV_W):
            off = HALO - (CONV_W - 1) + k
            base, sh = (off // SUBLANES) * SUBLANES, off % SUBLANES
            rows = slice(r0 + base, r0 + base + conv_rows)
            tap = zbuf[rows, :] if sh == 0 else zshift[sh - 1, rows, :]
            acc = acc + tap * dww_ref[k:k + 1, :]
        a = _layer_norm_rows(acc, clg_ref[...], clb_ref[...])
        abuf[r0:r0 + conv_rows, :] = (a * _sigmoid(a)).astype(BF16)
    zbuf[0:HALO, :] = zbuf[ts:ts + HALO, :]
    y_a = jnp.dot(abuf[...], wco_ref[...], preferred_element_type=F32)
    merged = _sigmoid(proj(4)) * y_a

    vbuf[...] = _layer_norm_rows(proj(3), slg_ref[...], slb_ref[...]).astype(BF16)
    pos_chunk_i = lax.broadcasted_iota(jnp.int32, (SGU_P, SGU_P), 0) // CHUNK
    pos_chunk_j = lax.broadcasted_iota(jnp.int32, (SGU_P, SGU_P), 1) // CHUNK
    keep = pos_chunk_j <= pos_chunk_i
    for g in range(SGU_G):
        wm = jnp.where(keep, wsp_ref[g], 0.0).astype(BF16)
        cols = slice(g * SGU_P, (g + 1) * SGU_P)
        for blk in range(ts // SGU_P):
            rows = slice(blk * SGU_P, (blk + 1) * SGU_P)
            mixed = jnp.dot(wm, vbuf[rows, cols], preferred_element_type=F32)
            mbuf[rows, cols] = mixed + bsp_ref[:, cols]
    gated = (proj(2) * mbuf[...]).astype(BF16)
    y_b = jnp.dot(gated, wso_ref[...], preferred_element_type=F32)
    merged = (merged + _sigmoid(proj(5)) * y_b).astype(BF16)

    o1 = jnp.dot(merged, wout_ref[...], preferred_element_type=F32)
    h_ref[0] = x + gate1 * o1


def _sub1(x, mod3, g1, win, dww, dwb, clg, clb, wco, slg, slb, wsp, bsp_full, wso, wout,
          ts=256, conv_rows=32):
    bsz, seq, _ = x.shape
    row = lambda a: a.reshape(1, D)
    kern = functools.partial(_sub1_kernel, conv_rows=conv_rows)
    return pl.pallas_call(
        kern,
        grid=(bsz, seq // ts),
        in_specs=[pl.BlockSpec((1, ts, D), lambda b, s: (b, s, 0)),
                  pl.BlockSpec((1, 6, D), lambda b, s: (b, 0, 0)),
                  _const_spec((1, D)),
                  _const_spec((D, 6 * D)),
                  _const_spec((CONV_W, D)),
                  _const_spec((1, D)),
                  _const_spec((1, D)),
                  _const_spec((1, D)),
                  _const_spec((D, D)),
                  _const_spec((1, D)),
                  _const_spec((1, D)),
                  _const_spec((SGU_G, SGU_P, SGU_P)),
                  _const_spec((SGU_P, D)),
                  _const_spec((D, D)),
                  _const_spec((D, D))],
        out_specs=pl.BlockSpec((1, ts, D), lambda b, s: (b, s, 0)),
        out_shape=jax.ShapeDtypeStruct((bsz, seq, D), F32),
        scratch_shapes=[pltpu.VMEM((HALO + ts + SUBLANES, D), F32),
                        pltpu.VMEM((SUBLANES - 1, HALO + ts, D), F32),
                        pltpu.VMEM((ts, D), BF16),
                        pltpu.VMEM((ts, D), BF16),
                        pltpu.VMEM((ts, D), F32)],
        compiler_params=pltpu.CompilerParams(dimension_semantics=("arbitrary", "arbitrary"),
                                             vmem_limit_bytes=VMEM_LIMIT),
        name="sub1",
    )(x, mod3, row(g1), win, dww, row(dwb), row(clg), row(clb), wco, row(slg), row(slb),
      wsp, bsp_full, wso, wout)


def _tree(op, xs):
    xs = list(xs)
    while len(xs) > 1:
        nxt = [op(xs[i], xs[i + 1]) for i in range(0, len(xs) - 1, 2)]
        if len(xs) % 2:
            nxt.append(xs[-1])
        xs = nxt
    return xs[0]


def _extract_topk(fill_work, work, rank, vals, n):
    def key_rows(k):
        return pl.ds(k * SUBLANES, SUBLANES)

    def run(lowest_index_first):
        fill_work()
        for k in range(n):
            rank[key_rows(k), :] = jnp.full((SUBLANES, LANES), float(TOPK), F32)

        def round_body(r, carry):
            s = [work[key_rows(k), :] for k in range(n)]
            m = _tree(jnp.maximum, s)
            vals[pl.ds(pl.multiple_of(r * SUBLANES, SUBLANES), SUBLANES), :] = m
            rf = lax.convert_element_type(r, F32)
            if lowest_index_first:
                first = _tree(jnp.minimum,
                              [jnp.where(s[k] == m, float(k), float(n)) for k in range(n)])
            for k in range(n):
                hit = (first == float(k)) if lowest_index_first else (s[k] == m)
                work[key_rows(k), :] = jnp.where(hit, NEG_INF, s[k])
                rank[key_rows(k), :] = jnp.where(hit, rf, rank[key_rows(k), :])
            return carry

        lax.fori_loop(0, TOPK, round_body, 0)

    run(lowest_index_first=False)
    ranked = _tree(jnp.add, [jnp.where(rank[key_rows(k), :] < float(TOPK), 1.0, 0.0)
                             for k in range(n)])
    tie_seen = jnp.max(jnp.abs(ranked - float(TOPK))) > 0.0

    @pl.when(tie_seen)
    def _():
        run(lowest_index_first=True)


def _route_kernel(h_ref, mod_ref, g2_ref, wqt_ref, kbig_ref,
                  n2t_ref, c_ref, p1_ref, r2_ref, p2_ref,
                  s1_scr, s2_scr, w1, w2, rk1, rk2, v1, v2, o_c, o_p1, o_p2):
    tr = h_ref.shape[0]
    h = h_ref[...]
    shift2 = mod_ref[0, 3:4, :]
    scale2 = mod_ref[0, 4:5, :]
    rn = h * lax.rsqrt(jnp.mean(h * h, axis=-1, keepdims=True) + EPS) * g2_ref[...]
    n2 = rn * (1.0 + scale2) + shift2
    n2t = n2.T.astype(BF16)
    n2t_ref[...] = n2t
    qt = jnp.dot(wqt_ref[...], n2t, preferred_element_type=F32).astype(BF16)
    half = HEADS * NKEYS
    s1_scr[...] = jnp.dot(kbig_ref[0], qt[0:half], preferred_element_type=F32)
    s2_scr[...] = jnp.dot(kbig_ref[1], qt[half:2 * half], preferred_element_type=F32)

    cands = [(a, b) for a in range(TOPK) for b in range(TOPK) if (a + 1) * (b + 1) <= TOPK]

    def block_body(tb, carry):
        lane = pl.ds(pl.multiple_of(tb * LANES, LANES), LANES)
        def fill_w1():
            w1[...] = s1_scr[:, lane]

        def fill_w2():
            w2[...] = s2_scr[:, lane]

        _extract_topk(fill_w1, w1, rk1, v1, NKEYS)
        _extract_topk(fill_w2, w2, rk2, v2, NKEYS)

        def rows(k):
            return pl.ds(k * SUBLANES, SUBLANES)

        av = [v1[rows(a), :] for a in range(TOPK)]
        bv = [v2[rows(b), :] for b in range(TOPK)]
        cv = [av[a] + bv[b] for (a, b) in cands]
        nc = len(cv)
        z = None
        m0 = None
        for r in range(TOPK):
            m = _tree(jnp.maximum, cv)
            first = _tree(jnp.minimum,
                          [jnp.where(cv[j] == m, float(j), float(nc)) for j in range(nc)])
            cv = [jnp.where(first == float(j), NEG_INF, cv[j]) for j in range(nc)]
            if r == 0:
                m0 = m
                z = jnp.ones_like(m)
            else:
                z = z + jnp.exp(m - m0)
        inv_z = 1.0 / z
        cnt = []
        for a in range(TOPK):
            sel = [jnp.where(cv[j] == NEG_INF, 1.0, 0.0) for j, (ca, _) in enumerate(cands) if ca == a]
            cnt.append(_tree(jnp.add, sel))
        for k in range(NKEYS):
            rk = rk1[rows(k), :]
            ck = jnp.zeros((SUBLANES, LANES), F32)
            for a in range(TOPK):
                ck = jnp.where(rk == float(a), cnt[a], ck)
            o_c[rows(k), :] = ck
            o_p1[rows(k), :] = jnp.exp(s1_scr[rows(k), lane] - av[0]) * inv_z
            o_p2[rows(k), :] = jnp.exp(s2_scr[rows(k), lane] - bv[0])
        for hd in range(HEADS):
            sel_rows = pl.ds(hd, NKEYS, stride=HEADS)
            c_hd = o_c[sel_rows, :]
            p1_hd = o_p1[sel_rows, :]
            c_ref[hd, :, lane] = pltpu.pack_elementwise([c_hd, c_hd], packed_dtype=BF16)
            p1_ref[hd, :, lane] = pltpu.pack_elementwise([p1_hd, p1_hd], packed_dtype=BF16)
            even = pl.ds(hd, NKEYS // 2, stride=2 * HEADS)
            odd = pl.ds(hd + HEADS, NKEYS // 2, stride=2 * HEADS)
            r2_ref[hd, :, lane] = pltpu.pack_elementwise([rk2[even, :], rk2[odd, :]],
                                                         packed_dtype=BF16)
            p2_ref[hd, :, lane] = pltpu.pack_elementwise([o_p2[even, :], o_p2[odd, :]],
                                                         packed_dtype=BF16)
        return carry

    lax.fori_loop(0, tr // LANES, block_body, 0)


def _route(h2d, mod3, g2, wqt, kbig, seq, tr=512):
    t = h2d.shape[0]
    tbl = jax.ShapeDtypeStruct((HEADS, NKEYS, t), jnp.uint32)
    tbl_pair = jax.ShapeDtypeStruct((HEADS, NKEYS // 2, t), jnp.uint32)
    tbl_spec = pl.BlockSpec((HEADS, NKEYS, tr), lambda i: (0, 0, i))
    pair_spec = pl.BlockSpec((HEADS, NKEYS // 2, tr), lambda i: (0, 0, i))
    scr = lambda n: pltpu.VMEM((n * SUBLANES, LANES), F32)
    return pl.pallas_call(
        _route_kernel,
        grid=(t // tr,),
        in_specs=[pl.BlockSpec((tr, D), lambda i: (i, 0)),
                  pl.BlockSpec((1, 6, D), lambda i: ((i * tr) // seq, 0, 0)),
                  _const_spec((1, D)),
                  _const_spec((2 * HEADS * NKEYS, D)),
                  _const_spec((2, HEADS * NKEYS, HEADS * NKEYS))],
        out_specs=[pl.BlockSpec((D, tr), lambda i: (0, i)),
                   tbl_spec, tbl_spec, pair_spec, pair_spec],
        out_shape=[jax.ShapeDtypeStruct((D, t), BF16), tbl, tbl, tbl_pair, tbl_pair],
        scratch_shapes=[pltpu.VMEM((HEADS * NKEYS, tr), F32),
                        pltpu.VMEM((HEADS * NKEYS, tr), F32),
                        scr(NKEYS), scr(NKEYS), scr(NKEYS), scr(NKEYS), scr(TOPK), scr(TOPK),
                        scr(NKEYS), scr(NKEYS), scr(NKEYS)],
        compiler_params=pltpu.CompilerParams(dimension_semantics=("arbitrary",),
                                             vmem_limit_bytes=VMEM_LIMIT),
        name="route",
    )(h2d, mod3, g2.reshape(1, D), wqt, kbig)


EXPERT_HALF = SUBLANES * NKEYS
PACK = 2 * SUBLANES


def _routing_weights_times_gelu(tb, ils, a_ref, out_ref, c_ref, p1_ref, r2_ref, p2_ref):
    if True:
        lane = slice(tb * LANES, (tb + 1) * LANES)
        c8 = [c_ref[hd, :, lane] for hd in range(HEADS)]
        p8 = [p1_ref[hd, :, lane] for hd in range(HEADS)]
        for il in ils:
            cb = [pltpu.bitcast(jnp.broadcast_to(c8[hd][il:il + 1, :], (SUBLANES, LANES)), BF16)
                  for hd in range(HEADS)]
            pb = [pltpu.bitcast(jnp.broadcast_to(p8[hd][il:il + 1, :], (SUBLANES, LANES)), BF16)
                  for hd in range(HEADS)]
            for jb in range(NKEYS // PACK):
                rows = slice(il * NKEYS + jb * PACK, il * NKEYS + (jb + 1) * PACK)
                jrows = slice(jb * SUBLANES, (jb + 1) * SUBLANES)
                g = _gelu_tanh(a_ref[rows, lane].astype(BF16))
                w = jnp.zeros((PACK, LANES), BF16)
                for hd in range(HEADS):
                    r2 = pltpu.bitcast(r2_ref[hd, jrows, lane], BF16)
                    p2 = pltpu.bitcast(p2_ref[hd, jrows, lane], BF16)
                    w = w + jnp.where(r2 < cb[hd], p2 * pb[hd], jnp.zeros_like(p2))
                out_ref[rows, lane] = g * w


def _experts_kernel(n2t_ref, c_odd_ref, p1_odd_ref, r2_odd_ref, p2_odd_ref,
                    c_even_ref, p1_even_ref, r2_even_ref, p2_even_ref,
                    u_ref, vt_ref, h_ref, mod_ref, gf_ref, o_ref, acc, a0, a1, h0, h1,
                    *, steps_per_tile):
    g = pl.program_id(0)
    first_of_tile = g % steps_per_tile == 1

    @pl.when(g == 0)
    def _():
        a1[...] = jnp.zeros_like(a1)
        h0[...] = jnp.zeros_like(h0)
        h1[...] = jnp.zeros_like(h1)

    @pl.when(jnp.logical_or(g == 0, first_of_tile))
    def _():
        acc[...] = jnp.zeros_like(acc)

    tm = n2t_ref.shape[1]
    mxu_n = 2 * LANES
    row_pieces = 2
    piece_rows = EXPERT_HALF // row_pieces
    n_pieces = row_pieces * (tm // mxu_n)
    ew_per_tb = n_pieces // (tm // LANES)
    il_per_piece = SUBLANES // ew_per_tb

    def stage(e0, a_dst, a_src, h_dst, h_src, tables):
        pieces = [(r, n) for n in range(tm // mxu_n) for r in range(row_pieces)]
        for p, (r, n) in enumerate(pieces):
            rows = slice(r * piece_rows, (r + 1) * piece_rows)
            cols = slice(n * mxu_n, (n + 1) * mxu_n)
            a_dst[rows, cols] = jnp.dot(u_ref[e0 + r * piece_rows:e0 + (r + 1) * piece_rows, :],
                                        n2t_ref[:, cols], preferred_element_type=F32)
            ils = range((p % ew_per_tb) * il_per_piece, (p % ew_per_tb + 1) * il_per_piece)
            _routing_weights_times_gelu(p // ew_per_tb, ils, a_src, h_dst, *tables)
            acc[rows, cols] += jnp.dot(vt_ref[rows, e0:e0 + EXPERT_HALF], h_src[:, cols],
                                       preferred_element_type=F32)

    stage(0, a0, a1, h1, h0, (c_odd_ref, p1_odd_ref, r2_odd_ref, p2_odd_ref))
    stage(EXPERT_HALF, a1, a0, h0, h1, (c_even_ref, p1_even_ref, r2_even_ref, p2_even_ref))

    @pl.when(jnp.logical_and(g % steps_per_tile == 0, g > 0))
    def _():
        gate2 = mod_ref[0, 5:6, :]
        h2 = h_ref[...] + gate2 * acc[...].T
        o_ref[...] = (h2 * lax.rsqrt(jnp.mean(h2 * h2, axis=-1, keepdims=True) + EPS)
                      * gf_ref[...])


def _experts(n2t, c_t, p1_t, r2_t, p2_t, u_bf, vt_bf, h2d, mod3, gf, seq, tm=512):
    t = h2d.shape[0]
    te = 2 * EXPERT_HALF
    nq = NEXP // te
    nt = t // tm
    groups = NKEYS // SUBLANES
    assert groups == 2 * nq
    last_half = nt * groups - 1

    def odd_half(g):
        return jnp.maximum(2 * g - 1, 0)

    def even_half(g):
        return jnp.minimum(2 * g, last_half)

    def out_tile(g):
        return jnp.maximum(g - 1, 0) // nq

    rows_spec = lambda half: pl.BlockSpec(
        (HEADS, SUBLANES, tm), lambda g: (0, half(g) % groups, half(g) // groups))
    pair_spec = lambda half: pl.BlockSpec(
        (HEADS, NKEYS // 2, tm), lambda g: (0, 0, half(g) // groups))
    kern = functools.partial(_experts_kernel, steps_per_tile=nq)
    return pl.pallas_call(
        kern,
        grid=(nt * nq + 1,),
        in_specs=[pl.BlockSpec((D, tm), lambda g: (0, jnp.minimum(g // nq, nt - 1))),
                  rows_spec(odd_half), rows_spec(odd_half), pair_spec(odd_half), pair_spec(odd_half),
                  rows_spec(even_half), rows_spec(even_half), pair_spec(even_half),
                  pair_spec(even_half),
                  pl.BlockSpec((te, D), lambda g: (g % nq, 0)),
                  pl.BlockSpec((D, te), lambda g: (0, jnp.maximum(g - 1, 0) % nq)),
                  pl.BlockSpec((tm, D), lambda g: (out_tile(g), 0)),
                  pl.BlockSpec((1, 6, D), lambda g: ((out_tile(g) * tm) // seq, 0, 0)),
                  _const_spec((1, D))],
        out_specs=pl.BlockSpec((tm, D), lambda g: (out_tile(g), 0)),
        out_shape=jax.ShapeDtypeStruct((t, D), F32),
        scratch_shapes=[pltpu.VMEM((D, tm), F32),
                        pltpu.VMEM((EXPERT_HALF, tm), F32),
                        pltpu.VMEM((EXPERT_HALF, tm), F32),
                        pltpu.VMEM((EXPERT_HALF, tm), BF16),
                        pltpu.VMEM((EXPERT_HALF, tm), BF16)],
        compiler_params=pltpu.CompilerParams(dimension_semantics=("arbitrary",),
                                             vmem_limit_bytes=VMEM_LIMIT),
        name="experts",
    )(n2t, c_t, p1_t, r2_t, p2_t, c_t, p1_t, r2_t, p2_t, u_bf, vt_bf, h2d, mod3,
      gf.reshape(1, D))


def _expand_keys(sub_keys):
    h, _, k, dh = sub_keys.shape
    eye = jnp.eye(h, dtype=sub_keys.dtype)
    big = jnp.einsum("hpkd,hg->pkhgd", sub_keys, eye)
    return big.reshape(2, k * h, h * dh)


def kernel(x, c, w_ada, b_ada, g_norm1, w_in, conv_dw_w, conv_dw_b, conv_ln_g, conv_ln_b,
           w_conv_out, sgu_ln_g, sgu_ln_b, w_spatial, b_spatial, w_sgu_out, w_out, g_norm2,
           w_query, sub_keys, expert_u, expert_v, g_final):
    bsz, seq, _ = x.shape
    depth = w_ada.shape[0]
    h = x
    for l in range(depth):
        mod3 = _adaln(c, w_ada[l], b_ada[l]).reshape(bsz, 6, D)
        bsp_full = jnp.repeat(b_spatial[l].T, SGU_P, axis=1)
        h = _sub1(h, mod3, g_norm1[l], w_in[l].astype(BF16), conv_dw_w[l], conv_dw_b[l],
                  conv_ln_g[l], conv_ln_b[l], w_conv_out[l].astype(BF16), sgu_ln_g[l],
                  sgu_ln_b[l], w_spatial[l], bsp_full, w_sgu_out[l].astype(BF16),
                  w_out[l].astype(BF16))
        wqt = (w_query[l].reshape(D, HEADS, 2, NKEYS).transpose(2, 1, 3, 0)
               .reshape(2 * HEADS * NKEYS, D).astype(BF16))
        kbig = _expand_keys(sub_keys[l]).astype(BF16)
        h2d = h.reshape(bsz * seq, D)
        n2t, c_t, p1_t, r2_t, p2_t = _route(h2d, mod3, g_norm2[l], wqt, kbig, seq)
        last = l == depth - 1
        gf = g_final if last else jnp.ones_like(g_final)
        out = _experts(n2t, c_t, p1_t, r2_t, p2_t, expert_u[l].astype(BF16),
                       expert_v[l].astype(BF16).T, h2d, mod3, gf, seq)
        h = out.reshape(bsz, seq, D)
    return h
```

```python
import functools

import jax
import jax.numpy as jnp
from jax import lax
from jax.experimental import pallas as pl
from jax.experimental.pallas import tpu as pltpu

F32 = jnp.float32
BF16 = jnp.bfloat16

D = 1024
EPS = 1e-6
CONV_W = 31
HALO = 32
SGU_G = 8
SGU_P = 128
CHUNK = 64
HEADS = 8
NKEYS = 128
TOPK = 16
NEXP = NKEYS * NKEYS
LANES = 128
SUBLANES = 8
VMEM_LIMIT = 56 * 1024 * 1024

NEG_INF = float("-inf")


def _sigmoid(x):
    return 1.0 / (1.0 + jnp.exp(-x))


def _gelu_tanh(x):
    k = 0.7978845608028654
    t = jnp.tanh(x * (k + (k * 0.044715) * (x * x)))
    return x * (0.5 + 0.5 * t)


def _const_spec(shape):
    nd = len(shape)
    return pl.BlockSpec(shape, lambda *_: (0,) * nd, pipeline_mode=pl.Buffered(1))


def _adaln_kernel(c_ref, w_ref, b_ref, o_ref):
    c = c_ref[...]
    ca = c * _sigmoid(c)
    o_ref[...] = jnp.dot(ca, w_ref[...], precision=lax.Precision.HIGHEST,
                         preferred_element_type=F32) + b_ref[...]


def _adaln(c, w, b, tn=1536):
    bsz, n = c.shape[0], w.shape[1]
    return pl.pallas_call(
        _adaln_kernel,
        grid=(n // tn,),
        in_specs=[pl.BlockSpec((bsz, D), lambda j: (0, 0)),
                  pl.BlockSpec((D, tn), lambda j: (0, j)),
                  pl.BlockSpec((1, tn), lambda j: (0, j))],
        out_specs=pl.BlockSpec((bsz, tn), lambda j: (0, j)),
        out_shape=jax.ShapeDtypeStruct((bsz, n), F32),
        compiler_params=pltpu.CompilerParams(dimension_semantics=("arbitrary",),
                                             vmem_limit_bytes=VMEM_LIMIT),
        name="adaln",
    )(c, w, b.reshape(1, n))


def _layer_norm_rows(x, g, b):
    mu = jnp.mean(x, axis=-1, keepdims=True)
    xc = x - mu
    var = jnp.mean(xc * xc, axis=-1, keepdims=True)
    return xc * lax.rsqrt(var + EPS) * g + b


def _sub1_kernel(x_ref, mod_ref, g1_ref, win_ref, dww_ref, dwb_ref, clg_ref, clb_ref, wco_ref,
                 slg_ref, slb_ref, wsp_ref, bsp_ref, wso_ref, wout_ref, h_ref,
                 zbuf, zshift, abuf, vbuf, mbuf, *, conv_rows):
    ts = x_ref.shape[1]
    s = pl.program_id(1)

    @pl.when(s == 0)
    def _():
        zbuf[0:HALO, :] = jnp.zeros((HALO, D), F32)
        zbuf[HALO + ts:HALO + ts + SUBLANES, :] = jnp.zeros((SUBLANES, D), F32)

    x = x_ref[0]
    shift1 = mod_ref[0, 0:1, :]
    scale1 = mod_ref[0, 1:2, :]
    gate1 = mod_ref[0, 2:3, :]
    rn = x * lax.rsqrt(jnp.mean(x * x, axis=-1, keepdims=True) + EPS) * g1_ref[...]
    n = (rn * (1.0 + scale1) + shift1).astype(BF16)

    def proj(c):
        return jnp.dot(n, win_ref[:, c * D:(c + 1) * D], preferred_element_type=F32)

    zbuf[HALO:HALO + ts, :] = proj(0) * _sigmoid(proj(1))

    for sh in range(1, SUBLANES):
        zshift[sh - 1] = zbuf[sh:sh + HALO + ts, :]

    for ci in range(ts // conv_rows):
        r0 = ci * conv_rows
        acc = jnp.broadcast_to(dwb_ref[...], (conv_rows, D))
        for k in range(CONV_W):
            off = HALO - (CONV_W - 1) + k
            base, sh = (off // SUBLANES) * SUBLANES, off % SUBLANES
            rows = slice(r0 + base, r0 + base + conv_rows)
            tap = zbuf[rows, :] if sh == 0 else zshift[sh - 1, rows, :]
            acc = acc + tap * dww_ref[k:k + 1, :]
        a = _layer_norm_rows(acc, clg_ref[...], clb_ref[...])
        abuf[r0:r0 + conv_rows, :] = (a * _sigmoid(a)).astype(BF16)
    zbuf[0:HALO, :] = zbuf[ts:ts + HALO, :]
    y_a = jnp.dot(abuf[...], wco_ref[...], preferred_element_type=F32)
    merged = _sigmoid(proj(4)) * y_a

    vbuf[...] = _layer_norm_rows(proj(3), slg_ref[...], slb_ref[...]).astype(BF16)
    pos_chunk_i = lax.broadcasted_iota(jnp.int32, (SGU_P, SGU_P), 0) // CHUNK
    pos_chunk_j = lax.broadcasted_iota(jnp.int32, (SGU_P, SGU_P), 1) // CHUNK
    keep = pos_chunk_j <= pos_chunk_i
    for g in range(SGU_G):
        wm = jnp.where(keep, wsp_ref[g], 0.0).astype(BF16)
        cols = slice(g * SGU_P, (g + 1) * SGU_P)
        for blk in range(ts // SGU_P):
            rows = slice(blk * SGU_P, (blk + 1) * SGU_P)
            mixed = jnp.dot(wm, vbuf[rows, cols], preferred_element_type=F32)
            mbuf[rows, cols] = mixed + bsp_ref[:, cols]
    gated = (proj(2) * mbuf[...]).astype(BF16)
    y_b = jnp.dot(gated, wso_ref[...], preferred_element_type=F32)
    merged = (merged + _sigmoid(proj(5)) * y_b).astype(BF16)

    o1 = jnp.dot(merged, wout_ref[...], preferred_element_type=F32)
    h_ref[0] = x + gate1 * o1


def _sub1(x, mod3, g1, win, dww, dwb, clg, clb, wco, slg, slb, wsp, bsp_full, wso, wout,
          ts=256, conv_rows=32):
    bsz, seq, _ = x.shape
    row = lambda a: a.reshape(1, D)
    kern = functools.partial(_sub1_kernel, conv_rows=conv_rows)
    return pl.pallas_call(
        kern,
        grid=(bsz, seq // ts),
        in_specs=[pl.BlockSpec((1, ts, D), lambda b, s: (b, s, 0)),
                  pl.BlockSpec((1, 6, D), lambda b, s: (b, 0, 0)),
                  _const_spec((1, D)),
                  _const_spec((D, 6 * D)),
                  _const_spec((CONV_W, D)),
                  _const_spec((1, D)),
                  _const_spec((1, D)),
                  _const_spec((1, D)),
                  _const_spec((D, D)),
                  _const_spec((1, D)),
                  _const_spec((1, D)),
                  _const_spec((SGU_G, SGU_P, SGU_P)),
                  _const_spec((SGU_P, D)),
                  _const_spec((D, D)),
                  _const_spec((D, D))],
        out_specs=pl.BlockSpec((1, ts, D), lambda b, s: (b, s, 0)),
        out_shape=jax.ShapeDtypeStruct((bsz, seq, D), F32),
        scratch_shapes=[pltpu.VMEM((HALO + ts + SUBLANES, D), F32),
                        pltpu.VMEM((SUBLANES - 1, HALO + ts, D), F32),
                        pltpu.VMEM((ts, D), BF16),
                        pltpu.VMEM((ts, D), BF16),
                        pltpu.VMEM((ts, D), F32)],
        compiler_params=pltpu.CompilerParams(dimension_semantics=("arbitrary", "arbitrary"),
                                             vmem_limit_bytes=VMEM_LIMIT),
        name="sub1",
    )(x, mod3, row(g1), win, dww, row(dwb), row(clg), row(clb), wco, row(slg), row(slb),
      wsp, bsp_full, wso, wout)


def _tree(op, xs):
    xs = list(xs)
    while len(xs) > 1:
        nxt = [op(xs[i], xs[i + 1]) for i in range(0, len(xs) - 1, 2)]
        if len(xs) % 2:
            nxt.append(xs[-1])
        xs = nxt
    return xs[0]


def _extract_topk(fill_work, work, rank, vals, n):
    def key_rows(k):
        return pl.ds(k * SUBLANES, SUBLANES)

    def run(lowest_index_first):
        fill_work()
        for k in range(n):
            rank[key_rows(k), :] = jnp.full((SUBLANES, LANES), float(TOPK), F32)

        def round_body(r, carry):
            s = [work[key_rows(k), :] for k in range(n)]
            m = _tree(jnp.maximum, s)
            vals[pl.ds(pl.multiple_of(r * SUBLANES, SUBLANES), SUBLANES), :] = m
            rf = lax.convert_element_type(r, F32)
            if lowest_index_first:
                first = _tree(jnp.minimum,
                              [jnp.where(s[k] == m, float(k), float(n)) for k in range(n)])
            for k in range(n):
                hit = (first == float(k)) if lowest_index_first else (s[k] == m)
                work[key_rows(k), :] = jnp.where(hit, NEG_INF, s[k])
                rank[key_rows(k), :] = jnp.where(hit, rf, rank[key_rows(k), :])
            return carry

        lax.fori_loop(0, TOPK, round_body, 0)

    run(lowest_index_first=False)
    ranked = _tree(jnp.add, [jnp.where(rank[key_rows(k), :] < float(TOPK), 1.0, 0.0)
                             for k in range(n)])
    tie_seen = jnp.max(jnp.abs(ranked - float(TOPK))) > 0.0

    @pl.when(tie_seen)
    def _():
        run(lowest_index_first=True)


def _route_kernel(h_ref, mod_ref, g2_ref, wqt_ref, kbig_ref,
                  n2t_ref, c_ref, p1_ref, r2_ref, p2_ref,
                  s1_scr, s2_scr, w1, w2, rk1, rk2, v1, v2, o_c, o_p1, o_p2):
    tr = h_ref.shape[0]
    h = h_ref[...]
    shift2 = mod_ref[0, 3:4, :]
    scale2 = mod_ref[0, 4:5, :]
    rn = h * lax.rsqrt(jnp.mean(h * h, axis=-1, keepdims=True) + EPS) * g2_ref[...]
    n2 = rn * (1.0 + scale2) + shift2
    n2t = n2.T.astype(BF16)
    n2t_ref[...] = n2t
    qt = jnp.dot(wqt_ref[...], n2t, preferred_element_type=F32).astype(BF16)
    half = HEADS * NKEYS
    s1_scr[...] = jnp.dot(kbig_ref[0], qt[0:half], preferred_element_type=F32)
    s2_scr[...] = jnp.dot(kbig_ref[1], qt[half:2 * half], preferred_element_type=F32)

    cands = [(a, b) for a in range(TOPK) for b in range(TOPK) if (a + 1) * (b + 1) <= TOPK]

    def block_body(tb, carry):
        lane = pl.ds(pl.multiple_of(tb * LANES, LANES), LANES)
        def fill_w1():
            w1[...] = s1_scr[:, lane]

        def fill_w2():
            w2[...] = s2_scr[:, lane]

        _extract_topk(fill_w1, w1, rk1, v1, NKEYS)
        _extract_topk(fill_w2, w2, rk2, v2, NKEYS)

        def rows(k):
            return pl.ds(k * SUBLANES, SUBLANES)

        av = [v1[rows(a), :] for a in range(TOPK)]
        bv = [v2[rows(b), :] for b in range(TOPK)]
        cv = [av[a] + bv[b] for (a, b) in cands]
        nc = len(cv)
        z = None
        m0 = None
        for r in range(TOPK):
            m = _tree(jnp.maximum, cv)
            first = _tree(jnp.minimum,
                          [jnp.where(cv[j] == m, float(j), float(nc)) for j in range(nc)])
            cv = [jnp.where(first == float(j), NEG_INF, cv[j]) for j in range(nc)]
            if r == 0:
                m0 = m
                z = jnp.ones_like(m)
            else:
                z = z + jnp.exp(m - m0)
        inv_z = 1.0 / z
        cnt = []
        for a in range(TOPK):
            sel = [jnp.where(cv[j] == NEG_INF, 1.0, 0.0) for j, (ca, _) in enumerate(cands) if ca == a]
            cnt.append(_tree(jnp.add, sel))
        for k in range(NKEYS):
            rk = rk1[rows(k), :]
            ck = jnp.zeros((SUBLANES, LANES), F32)
            for a in range(TOPK):
                ck = jnp.where(rk == float(a), cnt[a], ck)
            o_c[rows(k), :] = ck
            o_p1[rows(k), :] = jnp.exp(s1_scr[rows(k), lane] - av[0]) * inv_z
            o_p2[rows(k), :] = jnp.exp(s2_scr[rows(k), lane] - bv[0])
        for hd in range(HEADS):
            sel_rows = pl.ds(hd, NKEYS, stride=HEADS)
            c_hd = o_c[sel_rows, :]
            p1_hd = o_p1[sel_rows, :]
            c_ref[hd, :, lane] = pltpu.pack_elementwise([c_hd, c_hd], packed_dtype=BF16)
            p1_ref[hd, :, lane] = pltpu.pack_elementwise([p1_hd, p1_hd], packed_dtype=BF16)
            even = pl.ds(hd, NKEYS // 2, stride=2 * HEADS)
            odd = pl.ds(hd + HEADS, NKEYS // 2, stride=2 * HEADS)
            r2_ref[hd, :, lane] = pltpu.pack_elementwise([rk2[even, :], rk2[odd, :]],
                                                         packed_dtype=BF16)
            p2_ref[hd, :, lane] = pltpu.pack_elementwise([o_p2[even, :], o_p2[odd, :]],
                                                         packed_dtype=BF16)
        return carry

    lax.fori_loop(0, tr // LANES, block_body, 0)


def _route(h2d, mod3, g2, wqt, kbig, seq, tr=512):
    t = h2d.shape[0]
    tbl = jax.ShapeDtypeStruct((HEADS, NKEYS, t), jnp.uint32)
    tbl_pair = jax.ShapeDtypeStruct((HEADS, NKEYS // 2, t), jnp.uint32)
    tbl_spec = pl.BlockSpec((HEADS, NKEYS, tr), lambda i: (0, 0, i))
    pair_spec = pl.BlockSpec((HEADS, NKEYS // 2, tr), lambda i: (0, 0, i))
    scr = lambda n: pltpu.VMEM((n * SUBLANES, LANES), F32)
    return pl.pallas_call(
        _route_kernel,
        grid=(t // tr,),
        in_specs=[pl.BlockSpec((tr, D), lambda i: (i, 0)),
                  pl.BlockSpec((1, 6, D), lambda i: ((i * tr) // seq, 0, 0)),
                  _const_spec((1, D)),
                  _const_spec((2 * HEADS * NKEYS, D)),
                  _const_spec((2, HEADS * NKEYS, HEADS * NKEYS))],
        out_specs=[pl.BlockSpec((D, tr), lambda i: (0, i)),
                   tbl_spec, tbl_spec, pair_spec, pair_spec],
        out_shape=[jax.ShapeDtypeStruct((D, t), BF16), tbl, tbl, tbl_pair, tbl_pair],
        scratch_shapes=[pltpu.VMEM((HEADS * NKEYS, tr), F32),
                        pltpu.VMEM((HEADS * NKEYS, tr), F32),
                        scr(NKEYS), scr(NKEYS), scr(NKEYS), scr(NKEYS), scr(TOPK), scr(TOPK),
                        scr(NKEYS), scr(NKEYS), scr(NKEYS)],
        compiler_params=pltpu.CompilerParams(dimension_semantics=("arbitrary",),
                                             vmem_limit_bytes=VMEM_LIMIT),
        name="route",
    )(h2d, mod3, g2.reshape(1, D), wqt, kbig)


EXPERT_HALF = SUBLANES * NKEYS
PACK = 2 * SUBLANES


def _routing_weights_times_gelu(tb, ils, a_ref, out_ref, c_ref, p1_ref, r2_ref, p2_ref):
    if True:
        lane = slice(tb * LANES, (tb + 1) * LANES)
        c8 = [c_ref[hd, :, lane] for hd in range(HEADS)]
        p8 = [p1_ref[hd, :, lane] for hd in range(HEADS)]
        for il in ils:
            cb = [pltpu.bitcast(jnp.broadcast_to(c8[hd][il:il + 1, :], (SUBLANES, LANES)), BF16)
                  for hd in range(HEADS)]
            pb = [pltpu.bitcast(jnp.broadcast_to(p8[hd][il:il + 1, :], (SUBLANES, LANES)), BF16)
                  for hd in range(HEADS)]
            for jb in range(NKEYS // PACK):
                rows = slice(il * NKEYS + jb * PACK, il * NKEYS + (jb + 1) * PACK)
                jrows = slice(jb * SUBLANES, (jb + 1) * SUBLANES)
                g = _gelu_tanh(a_ref[rows, lane])
                w = jnp.zeros((PACK, LANES), BF16)
                for hd in range(HEADS):
                    r2 = pltpu.bitcast(r2_ref[hd, jrows, lane], BF16)
                    p2 = pltpu.bitcast(p2_ref[hd, jrows, lane], BF16)
                    w = w + jnp.where(r2 < cb[hd], p2 * pb[hd], jnp.zeros_like(p2))
                out_ref[rows, lane] = g * w


def _experts_kernel(n2t_ref, c_odd_ref, p1_odd_ref, r2_odd_ref, p2_odd_ref,
                    c_even_ref, p1_even_ref, r2_even_ref, p2_even_ref,
                    u_ref, vt_ref, h_ref, mod_ref, gf_ref, o_ref, acc, a0, a1, h0, h1,
                    *, steps_per_tile):
    g = pl.program_id(0)
    first_of_tile = g % steps_per_tile == 1

    @pl.when(g == 0)
    def _():
        a1[...] = jnp.zeros_like(a1)
        h0[...] = jnp.zeros_like(h0)
        h1[...] = jnp.zeros_like(h1)

    @pl.when(jnp.logical_or(g == 0, first_of_tile))
    def _():
        acc[...] = jnp.zeros_like(acc)

    tm = n2t_ref.shape[1]
    mxu_n = 2 * LANES
    row_pieces = 2
    piece_rows = EXPERT_HALF // row_pieces
    n_pieces = row_pieces * (tm // mxu_n)
    ew_per_tb = n_pieces // (tm // LANES)
    il_per_piece = SUBLANES // ew_per_tb

    def stage(e0, a_dst, a_src, h_dst, h_src, tables):
        pieces = [(r, n) for n in range(tm // mxu_n) for r in range(row_pieces)]
        for p, (r, n) in enumerate(pieces):
            rows = slice(r * piece_rows, (r + 1) * piece_rows)
            cols = slice(n * mxu_n, (n + 1) * mxu_n)
            a_dst[rows, cols] = jnp.dot(u_ref[e0 + r * piece_rows:e0 + (r + 1) * piece_rows, :],
                                        n2t_ref[:, cols],
                                        preferred_element_type=F32).astype(BF16)
            ils = range((p % ew_per_tb) * il_per_piece, (p % ew_per_tb + 1) * il_per_piece)
            _routing_weights_times_gelu(p // ew_per_tb, ils, a_src, h_dst, *tables)
            acc[rows, cols] += jnp.dot(vt_ref[rows, e0:e0 + EXPERT_HALF], h_src[:, cols],
                                       preferred_element_type=F32)

    stage(0, a0, a1, h1, h0, (c_odd_ref, p1_odd_ref, r2_odd_ref, p2_odd_ref))
    stage(EXPERT_HALF, a1, a0, h0, h1, (c_even_ref, p1_even_ref, r2_even_ref, p2_even_ref))

    @pl.when(jnp.logical_and(g % steps_per_tile == 0, g > 0))
    def _():
        gate2 = mod_ref[0, 5:6, :]
        h2 = h_ref[...] + gate2 * acc[...].T
        o_ref[...] = (h2 * lax.rsqrt(jnp.mean(h2 * h2, axis=-1, keepdims=True) + EPS)
                      * gf_ref[...])


def _experts(n2t, c_t, p1_t, r2_t, p2_t, u_bf, vt_bf, h2d, mod3, gf, seq, tm=512):
    t = h2d.shape[0]
    te = 2 * EXPERT_HALF
    nq = NEXP // te
    nt = t // tm
    groups = NKEYS // SUBLANES
    assert groups == 2 * nq
    last_half = nt * groups - 1

    def odd_half(g):
        return jnp.maximum(2 * g - 1, 0)

    def even_half(g):
        return jnp.minimum(2 * g, last_half)

    def out_tile(g):
        return jnp.maximum(g - 1, 0) // nq

    rows_spec = lambda half: pl.BlockSpec(
        (HEADS, SUBLANES, tm), lambda g: (0, half(g) % groups, half(g) // groups))
    pair_spec = lambda half: pl.BlockSpec(
        (HEADS, NKEYS // 2, tm), lambda g: (0, 0, half(g) // groups))
    kern = functools.partial(_experts_kernel, steps_per_tile=nq)
    return pl.pallas_call(
        kern,
        grid=(nt * nq + 1,),
        in_specs=[pl.BlockSpec((D, tm), lambda g: (0, jnp.minimum(g // nq, nt - 1))),
                  rows_spec(odd_half), rows_spec(odd_half), pair_spec(odd_half), pair_spec(odd_half),
                  rows_spec(even_half), rows_spec(even_half), pair_spec(even_half),
                  pair_spec(even_half),
                  pl.BlockSpec((te, D), lambda g: (g % nq, 0)),
                  pl.BlockSpec((D, te), lambda g: (0, jnp.maximum(g - 1, 0) % nq)),
                  pl.BlockSpec((tm, D), lambda g: (out_tile(g), 0)),
                  pl.BlockSpec((1, 6, D), lambda g: ((out_tile(g) * tm) // seq, 0, 0)),
                  _const_spec((1, D))],
        out_specs=pl.BlockSpec((tm, D), lambda g: (out_tile(g), 0)),
        out_shape=jax.ShapeDtypeStruct((t, D), F32),
        scratch_shapes=[pltpu.VMEM((D, tm), F32),
                        pltpu.VMEM((EXPERT_HALF, tm), BF16),
                        pltpu.VMEM((EXPERT_HALF, tm), BF16),
                        pltpu.VMEM((EXPERT_HALF, tm), BF16),
                        pltpu.VMEM((EXPERT_HALF, tm), BF16)],
        compiler_params=pltpu.CompilerParams(dimension_semantics=("arbitrary",),
                                             vmem_limit_bytes=VMEM_LIMIT),
        name="experts",
    )(n2t, c_t, p1_t, r2_t, p2_t, c_t, p1_t, r2_t, p2_t, u_bf, vt_bf, h2d, mod3,
      gf.reshape(1, D))


def _expand_keys(sub_keys):
    h, _, k, dh = sub_keys.shape
    eye = jnp.eye(h, dtype=sub_keys.dtype)
    big = jnp.einsum("hpkd,hg->pkhgd", sub_keys, eye)
    return big.reshape(2, k * h, h * dh)


def kernel(x, c, w_ada, b_ada, g_norm1, w_in, conv_dw_w, conv_dw_b, conv_ln_g, conv_ln_b,
           w_conv_out, sgu_ln_g, sgu_ln_b, w_spatial, b_spatial, w_sgu_out, w_out, g_norm2,
           w_query, sub_keys, expert_u, expert_v, g_final):
    bsz, seq, _ = x.shape
    depth = w_ada.shape[0]
    h = x
    for l in range(depth):
        mod3 = _adaln(c, w_ada[l], b_ada[l]).reshape(bsz, 6, D)
        bsp_full = jnp.repeat(b_spatial[l].T, SGU_P, axis=1)
        h = _sub1(h, mod3, g_norm1[l], w_in[l].astype(BF16), conv_dw_w[l], conv_dw_b[l],
                  conv_ln_g[l], conv_ln_b[l], w_conv_out[l].astype(BF16), sgu_ln_g[l],
                  sgu_ln_b[l], w_spatial[l], bsp_full, w_sgu_out[l].astype(BF16),
                  w_out[l].astype(BF16))
        wqt = (w_query[l].reshape(D, HEADS, 2, NKEYS).transpose(2, 1, 3, 0)
               .reshape(2 * HEADS * NKEYS, D).astype(BF16))
        kbig = _expand_keys(sub_keys[l]).astype(BF16)
        h2d = h.reshape(bsz * seq, D)
        n2t, c_t, p1_t, r2_t, p2_t = _route(h2d, mod3, g_norm2[l], wqt, kbig, seq)
        last = l == depth - 1
        gf = g_final if last else jnp.ones_like(g_final)
        out = _experts(n2t, c_t, p1_t, r2_t, p2_t, expert_u[l].astype(BF16),
                       expert_v[l].astype(BF16).T, h2d, mod3, gf, seq)
        h = out.reshape(bsz, seq, D)
    return h
```
